```python
import math
import jax, jax.numpy as jnp
from jax import lax
import numpy as np

D_MODEL = 1024
BATCH = 4
SEQ = 4096
DEPTH = 2

HEAD_DIM = 64
A_HEADS = D_MODEL // (4 * HEAD_DIM)
B_HEADS = D_MODEL // (2 * HEAD_DIM)
C_HEADS = D_MODEL // HEAD_DIM
A_QK = A_HEADS * 2 * HEAD_DIM
A_VW = A_HEADS * 2 * HEAD_DIM
B_W = B_HEADS * HEAD_DIM
C_W = C_HEADS * HEAD_DIM
EVEN_SPLITS = (A_QK, 2 * A_QK, 2 * A_QK + A_VW, 2 * A_QK + A_VW + B_W,
               2 * A_QK + A_VW + 2 * B_W, 2 * A_QK + A_VW + 3 * B_W)
EVEN_IN = 2 * A_QK + A_VW + 3 * B_W + B_HEADS
EVEN_OUT = A_VW + B_W
N_GROUPS = 4
N_EXPERTS = 8
TOP_K = 2
D_EXPERT = D_MODEL // 2
ROPE_THETA = 10000.0
Q_BLOCK = 128
RMS_EPS = 1e-6
FORGET_BIAS = 3.0
MAX_POS_OFFSET = 1024

kernel_name = 'hybrid_diff_fox_stickbreak_hmoe'


def rms_norm(x, g):
    xf = x.astype(jnp.float32)
    y = xf * lax.rsqrt(jnp.mean(xf * xf, axis=-1, keepdims=True) + RMS_EPS)
    return (y * g.astype(jnp.float32)).astype(x.dtype)


def rope(x, positions):
    d = x.shape[-1]
    half = d // 2
    inv = ROPE_THETA ** (-2.0 * jnp.arange(half, dtype=jnp.float32) / d)
    ang = positions.astype(jnp.float32)[..., None] * inv
    cos = jnp.cos(ang)[:, :, None, :]
    sin = jnp.sin(ang)[:, :, None, :]
    xf = x.astype(jnp.float32)
    x1, x2 = xf[..., :half], xf[..., half:]
    return jnp.concatenate([x1 * cos - x2 * sin, x2 * cos + x1 * sin], axis=-1).astype(x.dtype)


def causal_mask(i0, i1, strict):
    t = jnp.arange(i0, i1)[:, None]
    s = jnp.arange(i1)[None, :]
    return (t > s) if strict else (t >= s)


def sweep_query_blocks(block_fn, seq):
    return jnp.concatenate([block_fn(i0, i0 + Q_BLOCK) for i0 in range(0, seq, Q_BLOCK)], axis=1)


def diff_fox_mixer(h, positions, w_in, b_f, qn_a, kn_a, lam, subln_g, qn_b, kn_b, w_out, layer_idx):
    bsz, seq, _ = h.shape
    scale = HEAD_DIM ** -0.5
    proj = h @ w_in
    qa, ka, va, qb, kb, vb, fb = jnp.split(proj, list(EVEN_SPLITS), axis=-1)

    qa = rope(rms_norm(qa.reshape(bsz, seq, 2 * A_HEADS, HEAD_DIM), qn_a), positions)
    ka = rope(rms_norm(ka.reshape(bsz, seq, 2 * A_HEADS, HEAD_DIM), kn_a), positions)
    qa = qa.reshape(bsz, seq, A_HEADS, 2, HEAD_DIM)
    ka = ka.reshape(bsz, seq, A_HEADS, 2, HEAD_DIM)
    va = va.reshape(bsz, seq, A_HEADS, 2 * HEAD_DIM)
    lambda_init = 0.8 - 0.6 * math.exp(-0.3 * layer_idx)
    lam32 = lam.astype(jnp.float32)
    lam_full = (jnp.exp(jnp.sum(lam32[0] * lam32[1])) - jnp.exp(jnp.sum(lam32[2] * lam32[3]))
                + lambda_init)

    def diff_block(i0, i1):
        s = jnp.einsum('bqhcd,bkhcd->bhcqk', qa[:, i0:i1], ka[:, :i1]).astype(jnp.float32) * scale
        p = jax.nn.softmax(jnp.where(causal_mask(i0, i1, False), s, -jnp.inf), axis=-1)
        w = (p[:, :, 0] - lam_full * p[:, :, 1]).astype(va.dtype)
        return jnp.einsum('bhqk,bkhe->bqhe', w, va[:, :i1])

    o_a = sweep_query_blocks(diff_block, seq)
    o_a = rms_norm(o_a, subln_g) * (1.0 - lambda_init)

    qb = rms_norm(qb.reshape(bsz, seq, B_HEADS, HEAD_DIM), qn_b)
    kb = rms_norm(kb.reshape(bsz, seq, B_HEADS, HEAD_DIM), kn_b)
    vb = vb.reshape(bsz, seq, B_HEADS, HEAD_DIM)
    log_f = jax.nn.log_sigmoid((fb + b_f).astype(jnp.float32))
    cum_f = jnp.transpose(jnp.cumsum(log_f, axis=1), (0, 2, 1))

    def fox_block(i0, i1):
        s = (jnp.einsum('bqhd,bkhd->bhqk', qb[:, i0:i1], kb[:, :i1]).astype(jnp.float32) * scale
             + (cum_f[:, :, i0:i1, None] - cum_f[:, :, None, :i1]))
        p = jax.nn.softmax(jnp.where(causal_mask(i0, i1, False), s, -jnp.inf), axis=-1)
        return jnp.einsum('bhqk,bkhd->bqhd', p.astype(vb.dtype), vb[:, :i1])

    o_b = sweep_query_blocks(fox_block, seq)
    o = jnp.concatenate([o_a.reshape(bsz, seq, A_VW), o_b.reshape(bsz, seq, B_W)], axis=-1)
    return o @ w_out


def stick_breaking_mixer(h, w_in, w_out):
    bsz, seq, _ = h.shape
    scale = HEAD_DIM ** -0.5
    q, k, v = jnp.split(h @ w_in, 3, axis=-1)
    q = q.reshape(bsz, seq, C_HEADS, HEAD_DIM)
    k = k.reshape(bsz, seq, C_HEADS, HEAD_DIM)
    v = v.reshape(bsz, seq, C_HEADS, HEAD_DIM)

    def sb_block(i0, i1):
        z = jnp.einsum('bqhd,bkhd->bhqk', q[:, i0:i1], k[:, :i1]).astype(jnp.float32) * scale
        mask = causal_mask(i0, i1, True)
        log_beta = jax.nn.log_sigmoid(z)
        log_keep = jnp.where(mask, jax.nn.log_sigmoid(-z), 0.0)
        later = lax.cumsum(log_keep, axis=3, reverse=True) - log_keep
        a = jnp.where(mask, jnp.exp(log_beta + later), 0.0).astype(v.dtype)
        return jnp.einsum('bhqk,bkhd->bqhd', a, v[:, :i1])

    o = sweep_query_blocks(sb_block, seq)
    return o.reshape(bsz, seq, C_W) @ w_out


def hier_moe(h, w_gr, b_gr, w_er, b_er, w1, w3, w2):
    bsz, seq, d = h.shape
    hf = h.reshape(bsz * seq, d)
    g_prob = jax.nn.softmax((hf @ w_gr + b_gr).astype(jnp.float32), axis=-1)
    g_w, g_idx = lax.top_k(g_prob, 1)
    g_onehot = jax.nn.one_hot(g_idx[:, 0], N_GROUPS, dtype=jnp.float32)
    e_all = (jnp.einsum('nd,gde->nge', hf, w_er) + b_er).astype(jnp.float32)
    e_logits = jnp.einsum('nge,ng->ne', e_all, g_onehot)
    e_w, e_idx = lax.top_k(jax.nn.softmax(e_logits, axis=-1), TOP_K)
    e_w = e_w / jnp.sum(e_w, axis=-1, keepdims=True)
    within = jnp.einsum('nk,nke->ne', e_w, jax.nn.one_hot(e_idx, N_EXPERTS, dtype=jnp.float32))
    gate = ((g_onehot * g_w)[:, :, None] * within[:, None, :]).astype(h.dtype)
    y = jnp.zeros_like(hf)
    for g in range(N_GROUPS):
        hid = (jax.nn.silu(jnp.einsum('nd,edf->nef', hf, w1[g]))
               * jnp.einsum('nd,edf->nef', hf, w3[g]))
        y = y + jnp.einsum('nef,efd->nd', hid * gate[:, g, :, None], w2[g])
    return y.reshape(bsz, seq, d)


def setup_inputs(seed: int = 0) -> dict:
    key = jax.random.key(seed)
    ks = jax.random.split(key, 32)
    f32 = jnp.float32

    def nrm(k, shape, s):
        return jax.random.normal(k, shape, f32) * s

    n_even = (DEPTH + 1) // 2
    n_odd = DEPTH // 2
    D = D_MODEL
    positions = (jnp.arange(SEQ, dtype=jnp.int32)[None, :]
                 + jax.random.randint(ks[2], (BATCH, 1), 0, MAX_POS_OFFSET, dtype=jnp.int32))
    return {
        'x': nrm(ks[0], (BATCH, SEQ, D), 1.0),
        'c': nrm(ks[1], (BATCH, D), 1.0),
        'positions': positions,
        'mod_w': nrm(ks[3], (DEPTH, D, 6 * D), 0.5 * D ** -0.5),
        'mod_b': nrm(ks[4], (DEPTH, 6 * D), 0.02),
        'norm1_g': 1.0 + nrm(ks[5], (DEPTH, D), 0.02),
        'norm2_g': 1.0 + nrm(ks[6], (DEPTH, D), 0.02),
        'ev_w_in': nrm(ks[7], (n_even, D, EVEN_IN), D ** -0.5),
        'ev_b_f': FORGET_BIAS + nrm(ks[8], (n_even, B_HEADS), 0.5),
        'ev_qn_a': 1.0 + nrm(ks[9], (n_even, HEAD_DIM), 0.02),
        'ev_kn_a': 1.0 + nrm(ks[10], (n_even, HEAD_DIM), 0.02),
        'ev_lam': nrm(ks[11], (n_even, 4, HEAD_DIM), 0.1),
        'ev_subln_g': 1.0 + nrm(ks[12], (n_even, 2 * HEAD_DIM), 0.02),
        'ev_qn_b': 1.0 + nrm(ks[13], (n_even, HEAD_DIM), 0.02),
        'ev_kn_b': 1.0 + nrm(ks[14], (n_even, HEAD_DIM), 0.02),
        'ev_w_out': nrm(ks[15], (n_even, EVEN_OUT, D), EVEN_OUT ** -0.5),
        'od_w_in': nrm(ks[16], (n_odd, D, 3 * C_W), D ** -0.5),
        'od_w_out': nrm(ks[17], (n_odd, C_W, D), C_W ** -0.5),
        'moe_w_gr': nrm(ks[18], (DEPTH, D, N_GROUPS), D ** -0.5),
        'moe_b_gr': nrm(ks[19], (DEPTH, N_GROUPS), 0.01),
        'moe_w_er': nrm(ks[20], (DEPTH, N_GROUPS, D, N_EXPERTS), D ** -0.5),
        'moe_b_er': nrm(ks[21], (DEPTH, N_GROUPS, N_EXPERTS), 0.01),
        'moe_w1': nrm(ks[22], (DEPTH, N_GROUPS, N_EXPERTS, D, D_EXPERT), D ** -0.5),
        'moe_w3': nrm(ks[23], (DEPTH, N_GROUPS, N_EXPERTS, D, D_EXPERT), D ** -0.5),
        'moe_w2': nrm(ks[24], (DEPTH, N_GROUPS, N_EXPERTS, D_EXPERT, D), D_EXPERT ** -0.5),
    }


def reference(x, c, positions, mod_w, mod_b, norm1_g, norm2_g, ev_w_in, ev_b_f, ev_qn_a, ev_kn_a,
              ev_lam, ev_subln_g, ev_qn_b, ev_kn_b, ev_w_out, od_w_in, od_w_out, moe_w_gr, moe_b_gr,
              moe_w_er, moe_b_er, moe_w1, moe_w3, moe_w2):
    c_act = jax.nn.silu(c)
    for l in range(DEPTH):
        mod = c_act @ mod_w[l] + mod_b[l]
        shift1, scale1, gate1, shift2, scale2, gate2 = [m[:, None, :] for m in jnp.split(mod, 6, axis=-1)]
        h = rms_norm(x, norm1_g[l]) * (1.0 + scale1) + shift1
        if l % 2 == 0:
            e = l // 2
            mix = diff_fox_mixer(h, positions, ev_w_in[e], ev_b_f[e], ev_qn_a[e], ev_kn_a[e], ev_lam[e],
                                 ev_subln_g[e], ev_qn_b[e], ev_kn_b[e], ev_w_out[e], l)
        else:
            o = l // 2
            mix = stick_breaking_mixer(h, od_w_in[o], od_w_out[o])
        x = x + gate1 * mix
        h = rms_norm(x, norm2_g[l]) * (1.0 + scale2) + shift2
        x = x + gate2 * hier_moe(h, moe_w_gr[l], moe_b_gr[l], moe_w_er[l], moe_b_er[l],
                                 moe_w1[l], moe_w3[l], moe_w2[l])
    return x
```

```python
import functools
import math

import numpy as np
import jax
import jax.numpy as jnp
from jax import lax
from jax.experimental import pallas as pl
from jax.experimental.pallas import tpu as pltpu

F32 = jnp.float32
BF16 = jnp.bfloat16

D_MODEL = 1024
HEAD_DIM = 64
HALF = HEAD_DIM // 2
A_HEADS = 4
B_HEADS = 8
C_HEADS = 16
SEC = 512
N_GROUPS = 4
N_EXPERTS = 8
N_EXP_TOTAL = N_GROUPS * N_EXPERTS
D_EXPERT = D_MODEL // 2
ROPE_THETA = 10000.0
RMS_EPS = 1e-6
QK_SCALE = HEAD_DIM ** -0.5

LANES = 128
TM = 512
TQ = 512
TK = 512
SB_SUB = 256
T_EXP = 256
TC = 256
NEG = -1e30
ROUTE_E0 = 8

NT_DIMS = (((1,), (1,)), ((), ()))


def _dot(a, b):
    return jnp.dot(a, b, preferred_element_type=F32)


def _dot_nt(a, b):
    return lax.dot_general(a, b, NT_DIMS, preferred_element_type=F32)


def _split2(x):
    hi = x.astype(BF16)
    lo = (x - hi.astype(F32)).astype(BF16)
    return hi, lo


def _split3(x):
    hi = x.astype(BF16)
    r = x - hi.astype(F32)
    mid = r.astype(BF16)
    lo = (r - mid.astype(F32)).astype(BF16)
    return hi, mid, lo


def _softplus_neg_abs(z):
    return jnp.log1p(jnp.exp(-jnp.abs(z)))


def _log_sigmoid(z):
    return jnp.minimum(z, 0.0) - _softplus_neg_abs(z)


def _rms_rows(x, eps=RMS_EPS):
    return x * lax.rsqrt(jnp.mean(x * x, axis=-1, keepdims=True) + eps)


def _causal_pairs(n, descending):
    qi, ki = [], []
    for q in range(n):
        ks = range(q, -1, -1) if descending else range(q + 1)
        for k in ks:
            qi.append(q)
            ki.append(k)
    return np.asarray(qi, np.int32), np.asarray(ki, np.int32)


def _mod_kernel(c_ref, w_ref, b_ref, o_ref):
    c = c_ref[...]
    ca = c * jax.nn.sigmoid(c)
    c_hi, c_mid, c_lo = _split3(ca)
    w = w_ref[0]
    w_hi, w_lo = _split2(w)
    acc = _dot(c_hi, w_hi) + _dot(c_hi, w_lo) + _dot(c_mid, w_hi) + _dot(c_lo, w_hi) + _dot(c_mid, w_lo)
    o_ref[0] = acc + b_ref[0]


def _modulation(c, mod_w, mod_b):
    depth, d, n6 = mod_w.shape
    bsz = c.shape[0]
    rows = 8
    tn = 1536
    c_pad = jnp.zeros((rows, d), F32).at[:bsz].set(c)
    out = pl.pallas_call(
        _mod_kernel,
        grid=(depth, n6 // tn),
        in_specs=[
            pl.BlockSpec((rows, d), lambda l, j: (0, 0)),
            pl.BlockSpec((1, d, tn), lambda l, j: (l, 0, j)),
            pl.BlockSpec((1, 1, tn), lambda l, j: (l, 0, j)),
        ],
        out_specs=pl.BlockSpec((1, rows, tn), lambda l, j: (l, 0, j)),
        out_shape=jax.ShapeDtypeStruct((depth, rows, n6), F32),
        name="adaln_mod",
    )(c_pad, mod_w, mod_b.reshape(depth, 1, n6))
    return out[:, :bsz].reshape(depth, bsz, 6, d)


def _rope_table_kernel(pos_ref, inv_ref, cos_ref, sin_ref):
    ang = pos_ref[0].astype(F32) * inv_ref[...]
    cos_ref[0] = jnp.cos(ang)
    sin_ref[0] = jnp.sin(ang)


def _rope_tables(positions):
    bsz, seq = positions.shape
    ts = 2048
    inv = ROPE_THETA ** (-2.0 * jnp.arange(HALF, dtype=F32) / HEAD_DIM)
    return pl.pallas_call(
        _rope_table_kernel,
        grid=(bsz, seq // ts),
        in_specs=[
            pl.BlockSpec((1, 1, ts), lambda b, s: (b, 0, s)),
            pl.BlockSpec((HALF, 1), lambda b, s: (0, 0)),
        ],
        out_specs=[pl.BlockSpec((1, HALF, ts), lambda b, s: (b, 0, s))] * 2,
        out_shape=[jax.ShapeDtypeStruct((bsz, HALF, seq), F32)] * 2,
        name="rope_tables",
    )(positions.reshape(bsz, 1, seq), inv.reshape(HALF, 1))


def _adaln(x, mod_rows, g, first):
    shift = mod_rows[first:first + 1]
    scale = mod_rows[first + 1:first + 2]
    return _rms_rows(x) * g * (1.0 + scale) + shift


def _ln_proj_even_kernel(x_ref, mod_ref, g_ref, wT_ref, wfT_ref, wf_ref, bfc_ref, bfr_ref,
                         qna_ref, kna_ref, qnb_ref, knb_ref, cos_ref, sin_ref,
                         qa_o, ka_o, va_o, qb_o, kb_o, vb_o, cft_o, cfr_o,
                         carry_t, carry_r):
    tm = x_ref.shape[1]

    @pl.when(pl.program_id(1) == 0)
    def _():
        carry_t[...] = jnp.zeros_like(carry_t)
        carry_r[...] = jnp.zeros_like(carry_r)

    hb = _adaln(x_ref[0], mod_ref[0], g_ref[...], 0).astype(BF16)
    cos = cos_ref[0]
    sin = sin_ref[0]

    def section(idx):
        return _dot_nt(wT_ref[idx * SEC:(idx + 1) * SEC, :], hb)

    def norm_heads(p, g_col, out_ref, rope, scale):
        for j in range(SEC // HEAD_DIM):
            xj = p[j * HEAD_DIM:(j + 1) * HEAD_DIM]
            yj = xj * lax.rsqrt(jnp.mean(xj * xj, axis=0, keepdims=True) + RMS_EPS) * g_col
            if rope:
                y1 = yj[:HALF]
                y2 = yj[HALF:]
                o1 = y1 * cos - y2 * sin
                o2 = y2 * cos + y1 * sin
                out_ref[0, j * HEAD_DIM:j * HEAD_DIM + HALF, :] = (o1 * scale).astype(out_ref.dtype)
                out_ref[0, j * HEAD_DIM + HALF:(j + 1) * HEAD_DIM, :] = (o2 * scale).astype(out_ref.dtype)
            else:
                out_ref[0, j * HEAD_DIM:(j + 1) * HEAD_DIM, :] = (yj * scale).astype(out_ref.dtype)

    norm_heads(section(0), qna_ref[...], qa_o, True, QK_SCALE)
    norm_heads(section(1), kna_ref[...], ka_o, True, 1.0)
    va_o[0] = section(2).astype(va_o.dtype)
    norm_heads(section(3), qnb_ref[...], qb_o, False, QK_SCALE)
    norm_heads(section(4), knb_ref[...], kb_o, False, 1.0)
    vb_o[0] = section(5).astype(vb_o.dtype)

    r = lax.broadcasted_iota(jnp.int32, (tm, tm), 0)
    c = lax.broadcasted_iota(jnp.int32, (tm, tm), 1)
    upper = jnp.where(r <= c, 1.0, 0.0).astype(BF16)
    lower = jnp.where(c <= r, 1.0, 0.0).astype(BF16)

    lf_t = _log_sigmoid(_dot_nt(wfT_ref[...], hb)[:B_HEADS] + bfc_ref[...])
    t_hi, t_mid, t_lo = _split3(lf_t)
    cum_t = _dot(t_hi, upper) + _dot(t_mid, upper) + _dot(t_lo, upper) + carry_t[...]
    cft_o[0] = cum_t
    carry_t[...] = cum_t[:, tm - 1:tm]

    lf_r = _log_sigmoid(_dot(hb, wf_ref[...]) + bfr_ref[...])
    r_hi, r_mid, r_lo = _split3(lf_r)
    cum_r = _dot(lower, r_hi) + _dot(lower, r_mid) + _dot(lower, r_lo) + carry_r[...]
    cfr_o[0] = cum_r
    carry_r[...] = cum_r[tm - 1:tm, :]


def _ln_proj_even(x, mod, g, w_in, b_f, qn_a, kn_a, qn_b, kn_b, cos_t, sin_t):
    bsz, seq, d = x.shape
    n_main = 6 * SEC
    wT = w_in[:, :n_main].T.astype(BF16)
    w_f = w_in[:, n_main:]
    wfT = jnp.zeros((16, d), F32).at[:B_HEADS].set(w_f.T).astype(BF16)
    wf = jnp.zeros((d, LANES), F32).at[:, :B_HEADS].set(w_f).astype(BF16)
    bfc = b_f.reshape(B_HEADS, 1)
    bfr = jnp.zeros((1, LANES), F32).at[0, :B_HEADS].set(b_f)
    col = lambda v: v.reshape(HEAD_DIM, 1)
    full = lambda shape: pl.BlockSpec(shape, lambda b, s: (0,) * len(shape))
    sec_spec = pl.BlockSpec((1, SEC, TM), lambda b, s: (b, 0, s))
    sec_shape = jax.ShapeDtypeStruct((bsz, SEC, seq), BF16)
    return pl.pallas_call(
        _ln_proj_even_kernel,
        grid=(bsz, seq // TM),
        in_specs=[
            pl.BlockSpec((1, TM, d), lambda b, s: (b, s, 0)),
            pl.BlockSpec((1, 6, d), lambda b, s: (b, 0, 0)),
            full((1, d)), full((n_main, d)), full((16, d)), full((d, LANES)),
            full((B_HEADS, 1)), full((1, LANES)),
            full((HEAD_DIM, 1)), full((HEAD_DIM, 1)), full((HEAD_DIM, 1)), full((HEAD_DIM, 1)),
            pl.BlockSpec((1, HALF, TM), lambda b, s: (b, 0, s)),
            pl.BlockSpec((1, HALF, TM), lambda b, s: (b, 0, s)),
        ],
        out_specs=[sec_spec] * 6 + [
            pl.BlockSpec((1, B_HEADS, TM), lambda b, s: (b, 0, s)),
            pl.BlockSpec((1, TM, LANES), lambda b, s: (b, s, 0)),
        ],
        out_shape=[sec_shape] * 6 + [
            jax.ShapeDtypeStruct((bsz, B_HEADS, seq), F32),
            jax.ShapeDtypeStruct((bsz, seq, LANES), F32),
        ],
        scratch_shapes=[pltpu.VMEM((B_HEADS, 1), F32), pltpu.VMEM((1, LANES), F32)],
        name="ln_proj_even",
    )(x, mod, g.reshape(1, d), wT, wfT, wf, bfc, bfr, col(qn_a), col(kn_a), col(qn_b), col(kn_b),
      cos_t, sin_t)


def _ln_proj_odd_kernel(x_ref, mod_ref, g_ref, wT_ref, q_o, k_o, v_o):
    hb = _adaln(x_ref[0], mod_ref[0], g_ref[...], 0).astype(BF16)
    width = q_o.shape[1]
    for idx, (out_ref, scale) in enumerate(((q_o, QK_SCALE), (k_o, 1.0), (v_o, 1.0))):
        for half in range(2):
            lo = idx * width + half * (width // 2)
            p = _dot_nt(wT_ref[lo:lo + width // 2, :], hb)
            out_ref[0, half * (width // 2):(half + 1) * (width // 2), :] = (p * scale).astype(out_ref.dtype)


def _ln_proj_odd(x, mod, g, w_in):
    bsz, seq, d = x.shape
    width = C_HEADS * HEAD_DIM
    wT = w_in.T.astype(BF16)
    full = lambda shape: pl.BlockSpec(shape, lambda b, s: (0,) * len(shape))
    spec = pl.BlockSpec((1, width, TM), lambda b, s: (b, 0, s))
    shape = jax.ShapeDtypeStruct((bsz, width, seq), BF16)
    return pl.pallas_call(
        _ln_proj_odd_kernel,
        grid=(bsz, seq // TM),
        in_specs=[
            pl.BlockSpec((1, TM, d), lambda b, s: (b, s, 0)),
            pl.BlockSpec((1, 6, d), lambda b, s: (b, 0, 0)),
            full((1, d)), full((3 * width, d)),
        ],
        out_specs=[spec] * 3,
        out_shape=[shape] * 3,
        name="ln_proj_odd",
    )(x, mod, g.reshape(1, d), wT)


def _load_q_rows(qT_ref, q_s):
    qT = qT_ref[0].astype(F32)
    q_s[0] = qT[:HEAD_DIM].T.astype(q_s.dtype)
    q_s[1] = qT[HEAD_DIM:].T.astype(q_s.dtype)


def _online_softmax_step(s, v_t, m_ref, l_ref, acc_ref, idx):
    m_prev = m_ref[idx]
    m_new = jnp.maximum(m_prev, jnp.max(s, axis=-1, keepdims=True))
    alpha = jnp.exp(m_prev - m_new)
    p = jnp.exp(s - m_new)
    l_ref[idx] = alpha * l_ref[idx] + jnp.sum(p, axis=-1, keepdims=True)
    acc_ref[idx] = alpha * acc_ref[idx] + _dot_nt(p.astype(BF16), v_t)
    m_ref[idx] = m_new


def _attn_diff_kernel(qi_ref, ki_ref, qT_ref, kT_ref, vT_ref, lam_ref, sg_ref, o_ref,
                      q_s, m_s, l_s, acc_s, *, lambda_init):
    step = pl.program_id(2)
    qi = qi_ref[step]
    ki = ki_ref[step]
    tq = q_s.shape[1]
    tk = kT_ref.shape[2]

    @pl.when(ki == 0)
    def _():
        _load_q_rows(qT_ref, q_s)
        m_s[...] = jnp.full_like(m_s, -jnp.inf)
        l_s[...] = jnp.zeros_like(l_s)
        acc_s[...] = jnp.zeros_like(acc_s)

    row = qi * tq + lax.broadcasted_iota(jnp.int32, (tq, tk), 0)
    col = ki * tk + lax.broadcasted_iota(jnp.int32, (tq, tk), 1)
    valid = row >= col
    v_t = vT_ref[0]
    for c in range(2):
        s = _dot(q_s[c], kT_ref[0, c * HEAD_DIM:(c + 1) * HEAD_DIM, :])
        s = jnp.where(valid, s, NEG)
        _online_softmax_step(s, v_t, m_s, l_s, acc_s, c)

    @pl.when(ki == qi)
    def _():
        lam = lam_ref[...]
        e1 = jnp.exp(jnp.sum(lam[0:1] * lam[1:2], axis=-1, keepdims=True))
        e2 = jnp.exp(jnp.sum(lam[2:3] * lam[3:4], axis=-1, keepdims=True))
        lam_full = e1 - e2 + lambda_init
        o = acc_s[0] * (1.0 / l_s[0]) - lam_full * (acc_s[1] * (1.0 / l_s[1]))
        o = _rms_rows(o) * sg_ref[...] * (1.0 - lambda_init)
        o_ref[0] = o.astype(o_ref.dtype)


def _attn_fox_kernel(qi_ref, ki_ref, qT_ref, kT_ref, vT_ref, ck_ref, cq_ref, o_ref,
                     q_s, cq_s, m_s, l_s, acc_s):
    pair = pl.program_id(1)
    step = pl.program_id(2)
    qi = qi_ref[step]
    ki = ki_ref[step]
    tq = q_s.shape[1]
    tk = kT_ref.shape[2]

    @pl.when(ki == 0)
    def _():
        _load_q_rows(qT_ref, q_s)
        cq = cq_ref[0]
        lane = lax.broadcasted_iota(jnp.int32, cq.shape, 1)
        for j in range(2):
            cq_s[j] = jnp.sum(jnp.where(lane == 2 * pair + j, cq, 0.0), axis=-1, keepdims=True)
        m_s[...] = jnp.full_like(m_s, -jnp.inf)
        l_s[...] = jnp.zeros_like(l_s)
        acc_s[...] = jnp.zeros_like(acc_s)

    row = qi * tq + lax.broadcasted_iota(jnp.int32, (tq, tk), 0)
    col = ki * tk + lax.broadcasted_iota(jnp.int32, (tq, tk), 1)
    valid = row >= col
    ck = ck_ref[0, 0]
    for j in range(2):
        hs = slice(j * HEAD_DIM, (j + 1) * HEAD_DIM)
        s = _dot(q_s[j], kT_ref[0, hs, :]) + (cq_s[j] - ck[j:j + 1])
        s = jnp.where(valid, s, NEG)
        _online_softmax_step(s, vT_ref[0, hs, :], m_s, l_s, acc_s, j)

    @pl.when(ki == qi)
    def _():
        o = jnp.concatenate([acc_s[0] * (1.0 / l_s[0]), acc_s[1] * (1.0 / l_s[1])], axis=-1)
        o_ref[0] = o.astype(o_ref.dtype)


def _attn_sb_kernel(qi_ref, ki_ref, qT_ref, kT_ref, vT_ref, o_ref, q_s, r_s, acc_s):
    step = pl.program_id(2)
    qi = qi_ref[step]
    ki = ki_ref[step]
    tq = q_s.shape[1]
    tk = kT_ref.shape[2]

    @pl.when(ki == qi)
    def _():
        _load_q_rows(qT_ref, q_s)
        r_s[...] = jnp.zeros_like(r_s)
        acc_s[...] = jnp.zeros_like(acc_s)

    jr = lax.broadcasted_iota(jnp.int32, (SB_SUB, SB_SUB), 0)
    sc = lax.broadcasted_iota(jnp.int32, (SB_SUB, SB_SUB), 1)
    later_keys = jnp.where(jr > sc, 1.0, 0.0).astype(BF16)
    row = qi * tq + lax.broadcasted_iota(jnp.int32, (tq, SB_SUB), 0)
    col0 = lax.broadcasted_iota(jnp.int32, (tq, SB_SUB), 1)
    for j in range(2):
        hs = slice(j * HEAD_DIM, (j + 1) * HEAD_DIM)
        q = q_s[j]
        for sb in reversed(range(tk // SB_SUB)):
            ks = slice(sb * SB_SUB, (sb + 1) * SB_SUB)
            valid = row > (ki * tk + sb * SB_SUB + col0)
            z = _dot(q, kT_ref[0, hs, ks])
            sp = _softplus_neg_abs(z)
            log_beta = jnp.minimum(z, 0.0) - sp
            log_keep = jnp.where(valid, -jnp.maximum(z, 0.0) - sp, 0.0)
            k_hi, k_lo = _split2(log_keep)
            later = _dot(k_hi, later_keys) + _dot(k_lo, later_keys) + r_s[j]
            a = jnp.where(valid, jnp.exp(log_beta + later), 0.0)
            acc_s[j] = acc_s[j] + _dot_nt(a.astype(BF16), vT_ref[0, hs, ks])
            r_s[j] = later[:, 0:1] + log_keep[:, 0:1]

    @pl.when(ki == 0)
    def _():
        o_ref[0] = jnp.concatenate([acc_s[0], acc_s[1]], axis=-1).astype(o_ref.dtype)


def _attention_call(body, name, n_groups, qT, kT, vT, extra_in, extra_specs, scratch, descending):
    bsz, _, seq = qT.shape
    qi_tab, ki_tab = _causal_pairs(seq // TQ, descending)
    grp = 2 * HEAD_DIM
    in_specs = [
        pl.BlockSpec((1, grp, TQ), lambda b, h, p, qi, ki: (b, h, qi[p])),
        pl.BlockSpec((1, grp, TK), lambda b, h, p, qi, ki: (b, h, ki[p])),
        pl.BlockSpec((1, grp, TK), lambda b, h, p, qi, ki: (b, h, ki[p])),
    ] + extra_specs
    grid_spec = pltpu.PrefetchScalarGridSpec(
        num_scalar_prefetch=2,
        grid=(bsz, n_groups, len(qi_tab)),
        in_specs=in_specs,
        out_specs=pl.BlockSpec((1, TQ, grp), lambda b, h, p, qi, ki: (b, qi[p], h)),
        scratch_shapes=scratch,
    )
    return pl.pallas_call(
        body,
        grid_spec=grid_spec,
        out_shape=jax.ShapeDtypeStruct((bsz, seq, n_groups * grp), BF16),
        name=name,
    )(jnp.asarray(qi_tab), jnp.asarray(ki_tab), qT, kT, vT, *extra_in)


def _softmax_scratch(v_dim):
    return [
        pltpu.VMEM((2, TQ, HEAD_DIM), BF16),
        pltpu.VMEM((2, TQ, 1), F32),
        pltpu.VMEM((2, TQ, 1), F32),
        pltpu.VMEM((2, TQ, v_dim), F32),
    ]


def _even_mixer_attention(qa, ka, va, qb, kb, vb, cft, cfr, lam, subln_g, lambda_init):
    bsz, _, seq = qa.shape
    full = lambda shape: pl.BlockSpec(shape, lambda b, h, p, qi, ki: (0,) * len(shape))
    o_a = _attention_call(
        functools.partial(_attn_diff_kernel, lambda_init=lambda_init), "attn_diff", A_HEADS,
        qa, ka, va, [lam, subln_g.reshape(1, 2 * HEAD_DIM)],
        [full((4, HEAD_DIM)), full((1, 2 * HEAD_DIM))],
        _softmax_scratch(2 * HEAD_DIM), False)
    fox_scratch = _softmax_scratch(HEAD_DIM)
    fox_scratch.insert(1, pltpu.VMEM((2, TQ, 1), F32))
    o_b = _attention_call(
        _attn_fox_kernel, "attn_fox", B_HEADS // 2,
        qb, kb, vb, [cft.reshape(bsz, B_HEADS // 2, 2, seq), cfr],
        [pl.BlockSpec((1, 1, 2, TK), lambda b, h, p, qi, ki: (b, h, 0, ki[p])),
         pl.BlockSpec((1, TQ, LANES), lambda b, h, p, qi, ki: (b, qi[p], 0))],
        fox_scratch, False)
    return [o_a, o_b]


def _sb_attention(qT, kT, vT):
    scratch = [
        pltpu.VMEM((2, TQ, HEAD_DIM), BF16),
        pltpu.VMEM((2, TQ, 1), F32),
        pltpu.VMEM((2, TQ, HEAD_DIM), F32),
    ]
    return [_attention_call(_attn_sb_kernel, "attn_sb", C_HEADS // 2, qT, kT, vT, [], [], scratch, True)]


def _out_proj_route_kernel(*refs, n_parts):
    o_refs = refs[:n_parts]
    wo_refs = refs[n_parts:2 * n_parts]
    x_ref, mod_ref, g_ref, wrh_ref, wrl_ref, br_ref, x1_o, h2_o, ids_o, wts_o = refs[2 * n_parts:]
    mod_rows = mod_ref[0]
    gate1 = mod_rows[2:3]
    mix = _dot(o_refs[0][0], wo_refs[0][...])
    for o_ref, wo_ref in zip(o_refs[1:], wo_refs[1:]):
        mix = mix + _dot(o_ref[0], wo_ref[...])
    x1 = x_ref[0] + gate1 * mix
    x1_o[0] = x1
    h2 = _adaln(x1, mod_rows, g_ref[...], 3)
    h2_o[0] = h2

    h_hi, h_lo = _split2(h2)
    wr_hi = wrh_ref[...]
    logits = _dot(h_hi, wr_hi) + _dot(h_hi, wrl_ref[...]) + _dot(h_lo, wr_hi) + br_ref[...]

    lane = lax.broadcasted_iota(jnp.int32, logits.shape, 1)
    big = jnp.int32(LANES)
    g_mask = lane < N_GROUPS
    g_log = jnp.where(g_mask, logits, NEG)
    g_max = jnp.max(g_log, axis=-1, keepdims=True)
    g_sum = jnp.sum(jnp.where(g_mask, jnp.exp(g_log - g_max), 0.0), axis=-1, keepdims=True)
    g_w = 1.0 / g_sum
    g_idx = jnp.min(jnp.where(g_mask & (g_log == g_max), lane, big), axis=-1, keepdims=True)

    e_mask = (lane >= ROUTE_E0) & (lane < ROUTE_E0 + N_EXP_TOTAL) & ((lane >> 3) == g_idx + 1)
    e_log = jnp.where(e_mask, logits, NEG)
    e_max = jnp.max(e_log, axis=-1, keepdims=True)
    e_exp = jnp.where(e_mask, jnp.exp(e_log - e_max), 0.0)
    e_prob = e_exp / jnp.sum(e_exp, axis=-1, keepdims=True)
    p1 = jnp.max(jnp.where(e_mask, e_prob, -1.0), axis=-1, keepdims=True)
    i1 = jnp.min(jnp.where(e_mask & (e_prob == p1), lane, big), axis=-1, keepdims=True)
    rest = e_mask & (lane != i1)
    p2 = jnp.max(jnp.where(rest, e_prob, -1.0), axis=-1, keepdims=True)
    i2 = jnp.min(jnp.where(rest & (e_prob == p2), lane, big), axis=-1, keepdims=True)
    den = p1 + p2
    w1 = p1 / den * g_w
    w2 = p2 / den * g_w
    ids_o[0] = jnp.where(lane == 0, i1 - ROUTE_E0, jnp.where(lane == 1, i2 - ROUTE_E0, 0))
    wts_o[0] = jnp.where(lane == 0, w1, jnp.where(lane == 1, w2, 0.0))


def _out_proj_route(o_parts, x, mod, g2, w_out, w_gr, b_gr, w_er, b_er):
    bsz, seq, d = x.shape
    widths = [o.shape[-1] for o in o_parts]
    starts = np.cumsum([0] + widths)
    wo_parts = [w_out[starts[i]:starts[i + 1]].astype(BF16) for i in range(len(o_parts))]
    wr = jnp.zeros((d, LANES), F32)
    wr = wr.at[:, :N_GROUPS].set(w_gr)
    wr = wr.at[:, ROUTE_E0:ROUTE_E0 + N_EXP_TOTAL].set(
        jnp.transpose(w_er, (1, 0, 2)).reshape(d, N_EXP_TOTAL))
    br = jnp.zeros((1, LANES), F32)
    br = br.at[0, :N_GROUPS].set(b_gr)
    br = br.at[0, ROUTE_E0:ROUTE_E0 + N_EXP_TOTAL].set(b_er.reshape(N_EXP_TOTAL))
    wr_hi = wr.astype(BF16)
    wr_lo = (wr - wr_hi.astype(F32)).astype(BF16)
    full = lambda shape: pl.BlockSpec(shape, lambda b, s: (0,) * len(shape))
    tile = lambda w: pl.BlockSpec((1, TM, w), lambda b, s: (b, s, 0))
    return pl.pallas_call(
        functools.partial(_out_proj_route_kernel, n_parts=len(o_parts)),
        grid=(bsz, seq // TM),
        in_specs=[tile(w) for w in widths] + [full((w, d)) for w in widths] + [
            tile(d),
            pl.BlockSpec((1, 6, d), lambda b, s: (b, 0, 0)),
            full((1, d)), full((d, LANES)), full((d, LANES)), full((1, LANES)),
        ],
        out_specs=[tile(d), tile(d), tile(LANES), tile(LANES)],
        out_shape=[
            jax.ShapeDtypeStruct((bsz, seq, d), F32),
            jax.ShapeDtypeStruct((bsz, seq, d), F32),
            jax.ShapeDtypeStruct((bsz, seq, LANES), jnp.int32),
            jax.ShapeDtypeStruct((bsz, seq, LANES), F32),
        ],
        name="out_proj_route",
    )(*o_parts, *wo_parts, x, mod, g2.reshape(1, d), wr_hi, wr_lo, br)


def _expert_onehots(ids_ref):
    tc = ids_ref.shape[2]
    expert = lax.broadcasted_iota(jnp.int32, (N_EXP_TOTAL, tc), 0)
    ids = ids_ref[0]
    return [jnp.where(expert == ids[k:k + 1], 1.0, 0.0) for k in range(2)]


def _rank_kernel(ids_ref, rank_o, cnt_o, carry):
    tc = ids_ref.shape[2]

    @pl.when(pl.program_id(0) == 0)
    def _():
        carry[...] = jnp.zeros_like(carry)

    r = lax.broadcasted_iota(jnp.int32, (tc, tc), 0)
    c = lax.broadcasted_iota(jnp.int32, (tc, tc), 1)
    before = jnp.where(r < c, 1.0, 0.0).astype(BF16)
    base = carry[...]
    rank_o[0] = jnp.zeros(rank_o.shape[1:], rank_o.dtype)
    for k, oh in enumerate(_expert_onehots(ids_ref)):
        prefix = _dot(oh.astype(BF16), before)
        rank_o[0, k:k + 1, :] = jnp.sum(oh * (base + prefix), axis=0, keepdims=True).astype(jnp.int32)
        base = base + jnp.sum(oh, axis=-1, keepdims=True)
    carry[...] = base
    cnt_o[...] = jnp.broadcast_to(base, cnt_o.shape)


def _pos_kernel(ids_ref, rank_ref, off_ref, pos_o):
    rank = rank_ref[0]
    pos_o[0] = jnp.zeros(pos_o.shape[1:], pos_o.dtype)
    for k, oh in enumerate(_expert_onehots(ids_ref)):
        off = jnp.sum(oh * off_ref[...], axis=0, keepdims=True).astype(jnp.int32)
        pos_o[0, k:k + 1, :] = off + rank[k:k + 1]


def _dispatch_plan(ids2):
    n = ids2.shape[0]
    nc = n // TC
    ids_t = jnp.zeros((nc, 8, TC), jnp.int32).at[:, :2].set(
        jnp.transpose(ids2.reshape(nc, TC, 2), (0, 2, 1)))
    blk = pl.BlockSpec((1, 8, TC), lambda c: (c, 0, 0))
    rank, cnt = pl.pallas_call(
        _rank_kernel,
        grid=(nc,),
        in_specs=[blk],
        out_specs=[blk, pl.BlockSpec((N_EXP_TOTAL, LANES), lambda c: (0, 0))],
        out_shape=[jax.ShapeDtypeStruct((nc, 8, TC), jnp.int32),
                   jax.ShapeDtypeStruct((N_EXP_TOTAL, LANES), F32)],
        scratch_shapes=[pltpu.VMEM((N_EXP_TOTAL, 1), F32)],
        name="moe_rank",
    )(ids_t)
    counts = cnt[:, 0].astype(jnp.int32)
    tiles_per = (counts + T_EXP - 1) // T_EXP
    tile_start = jnp.cumsum(tiles_per) - tiles_per
    n_used = jnp.sum(tiles_per)
    pos = pl.pallas_call(
        _pos_kernel,
        grid=(nc,),
        in_specs=[blk, blk, pl.BlockSpec((N_EXP_TOTAL, 1), lambda c: (0, 0))],
        out_specs=blk,
        out_shape=jax.ShapeDtypeStruct((nc, 8, TC), jnp.int32),
        name="moe_pos",
    )(ids_t, rank, (tile_start * T_EXP).astype(F32).reshape(N_EXP_TOTAL, 1))
    n_tiles = (2 * n) // T_EXP + N_EXP_TOTAL
    tile_idx = jnp.minimum(jnp.arange(n_tiles, dtype=jnp.int32), n_used - 1)
    tile_expert = (jnp.searchsorted(tile_start, tile_idx, side="right") - 1).astype(jnp.int32)
    pos_chunks = pos[:, :2].reshape(nc, 1, 2 * TC)
    return pos_chunks, tile_expert, n_used.reshape(1).astype(jnp.int32), n_tiles


def _row_copy_wait(src_hbm, dst, sem, rows):
    pltpu.make_async_copy(src_hbm.at[pl.ds(0, rows)], dst.at[pl.ds(0, rows)], sem).wait()


def _dispatch_kernel(pos_ref, h_hbm, xs_in_hbm, xs_hbm, sem):
    del xs_in_hbm
    chunk = pl.program_id(0)

    def issue(i, carry):
        tok = chunk * TC + i
        for k in range(2):
            dst = pos_ref[0, 0, k * TC + i]
            pltpu.make_async_copy(h_hbm.at[pl.ds(tok, 1)], xs_hbm.at[pl.ds(dst, 1)], sem).start()
        return carry

    lax.fori_loop(0, TC, issue, 0)
    _row_copy_wait(h_hbm, xs_hbm, sem, 2 * TC)


def _dispatch(h2_flat, pos_chunks, n_rows):
    n, d = h2_flat.shape
    xs0 = jnp.zeros((n_rows, d), h2_flat.dtype)
    return pl.pallas_call(
        _dispatch_kernel,
        grid=(n // TC,),
        in_specs=[
            pl.BlockSpec((1, 1, 2 * TC), lambda c: (c, 0, 0), memory_space=pltpu.SMEM),
            pl.BlockSpec(memory_space=pl.ANY),
            pl.BlockSpec(memory_space=pl.ANY),
        ],
        out_specs=pl.BlockSpec(memory_space=pl.ANY),
        out_shape=jax.ShapeDtypeStruct((n_rows, d), h2_flat.dtype),
        scratch_shapes=[pltpu.SemaphoreType.DMA(())],
        input_output_aliases={2: 0},
        name="moe_dispatch",
    )(pos_chunks, h2_flat, xs0)


def _expert_kernel(te_ref, nu_ref, xs_ref, w1_ref, w3_ref, w2_ref, ys_ref, w1_s, w3_s, w2_s):
    j = pl.program_id(0)
    prev = te_ref[jnp.maximum(j - 1, 0)]

    @pl.when((j == 0) | (te_ref[j] != prev))
    def _():
        w1_s[...] = w1_ref[0].astype(BF16)
        w3_s[...] = w3_ref[0].astype(BF16)
        w2_s[...] = w2_ref[0].astype(BF16)

    @pl.when(j < nu_ref[0])
    def _():
        xb = xs_ref[...].astype(BF16)
        a = _dot(xb, w1_s[...])
        b = _dot(xb, w3_s[...])
        hid = a * jax.nn.sigmoid(a) * b
        ys_ref[...] = _dot(hid.astype(BF16), w2_s[...])

    @pl.when(j >= nu_ref[0])
    def _():
        ys_ref[...] = jnp.zeros_like(ys_ref)


def _experts(xs, tile_expert, n_used, w1, w3, w2):
    n_rows, d = xs.shape
    f = w1.shape[-1]
    grid_spec = pltpu.PrefetchScalarGridSpec(
        num_scalar_prefetch=2,
        grid=(n_rows // T_EXP,),
        in_specs=[
            pl.BlockSpec((T_EXP, d), lambda j, te, nu: (j, 0)),
            pl.BlockSpec((1, d, f), lambda j, te, nu: (te[j], 0, 0)),
            pl.BlockSpec((1, d, f), lambda j, te, nu: (te[j], 0, 0)),
            pl.BlockSpec((1, f, d), lambda j, te, nu: (te[j], 0, 0)),
        ],
        out_specs=pl.BlockSpec((T_EXP, d), lambda j, te, nu: (j, 0)),
        scratch_shapes=[pltpu.VMEM((d, f), BF16), pltpu.VMEM((d, f), BF16), pltpu.VMEM((f, d), BF16)],
    )
    return pl.pallas_call(
        _expert_kernel,
        grid_spec=grid_spec,
        out_shape=jax.ShapeDtypeStruct((n_rows, d), F32),
        name="moe_experts",
    )(tile_expert, n_used, xs, w1.reshape(N_EXP_TOTAL, d, f), w3.reshape(N_EXP_TOTAL, d, f),
      w2.reshape(N_EXP_TOTAL, f, d))


def _combine_kernel(pos_ref, ys_hbm, x1_ref, wts_ref, mod_ref, x2_o, buf, sem):
    def issue(i, carry):
        for k in range(2):
            src = pos_ref[0, 0, k * TC + i]
            pltpu.make_async_copy(ys_hbm.at[pl.ds(src, 1)], buf.at[k, pl.ds(i, 1)], sem).start()
        return carry

    lax.fori_loop(0, TC, issue, 0)
    for k in range(2):
        _row_copy_wait(ys_hbm, buf.at[k], sem, TC)
    wts = wts_ref[0]
    y = wts[:, 0:1] * buf[0] + wts[:, 1:2] * buf[1]
    gate2 = mod_ref[0][5:6]
    x2_o[0] = x1_ref[0] + gate2 * y


def _combine(ys, pos_chunks, x1, wts, mod):
    bsz, seq, d = x1.shape
    per_b = seq // TC
    return pl.pallas_call(
        _combine_kernel,
        grid=(bsz, per_b),
        in_specs=[
            pl.BlockSpec((1, 1, 2 * TC), lambda b, s: (b * per_b + s, 0, 0), memory_space=pltpu.SMEM),
            pl.BlockSpec(memory_space=pl.ANY),
            pl.BlockSpec((1, TC, d), lambda b, s: (b, s, 0)),
            pl.BlockSpec((1, TC, LANES), lambda b, s: (b, s, 0)),
            pl.BlockSpec((1, 6, d), lambda b, s: (b, 0, 0)),
        ],
        out_specs=pl.BlockSpec((1, TC, d), lambda b, s: (b, s, 0)),
        out_shape=jax.ShapeDtypeStruct((bsz, seq, d), F32),
        scratch_shapes=[pltpu.VMEM((2, TC, d), F32), pltpu.SemaphoreType.DMA(())],
        name="moe_combine",
    )(pos_chunks, ys, x1, wts, mod)


def _moe(h2, ids, wts, x1, mod, w1, w3, w2):
    bsz, seq, d = h2.shape
    n = bsz * seq
    pos_chunks, tile_expert, n_used, n_tiles = _dispatch_plan(ids.reshape(n, LANES)[:, :2])
    xs = _dispatch(h2.reshape(n, d), pos_chunks, n_tiles * T_EXP)
    ys = _experts(xs, tile_expert, n_used, w1, w3, w2)
    return _combine(ys, pos_chunks, x1, wts, mod)


def kernel(x, c, positions, mod_w, mod_b, norm1_g, norm2_g, ev_w_in, ev_b_f, ev_qn_a, ev_kn_a, ev_lam,
           ev_subln_g, ev_qn_b, ev_kn_b, ev_w_out, od_w_in, od_w_out, moe_w_gr, moe_b_gr, moe_w_er,
           moe_b_er, moe_w1, moe_w3, moe_w2):
    depth = mod_w.shape[0]
    mod = _modulation(c, mod_w, mod_b)
    cos_t, sin_t = _rope_tables(positions)
    for l in range(depth):
        if l % 2 == 0:
            e = l // 2
            lambda_init = 0.8 - 0.6 * math.exp(-0.3 * l)
            qa, ka, va, qb, kb, vb, cft, cfr = _ln_proj_even(
                x, mod[l], norm1_g[l], ev_w_in[e], ev_b_f[e], ev_qn_a[e], ev_kn_a[e], ev_qn_b[e],
                ev_kn_b[e], cos_t, sin_t)
            o = _even_mixer_attention(qa, ka, va, qb, kb, vb, cft, cfr, ev_lam[e], ev_subln_g[e],
                                      lambda_init)
            w_out = ev_w_out[e]
        else:
            od = l // 2
            q, k, v = _ln_proj_odd(x, mod[l], norm1_g[l], od_w_in[od])
            o = _sb_attention(q, k, v)
            w_out = od_w_out[od]
        x1, h2, ids, wts = _out_proj_route(o, x, mod[l], norm2_g[l], w_out, moe_w_gr[l], moe_b_gr[l],
                                           moe_w_er[l], moe_b_er[l])
        x = _moe(h2, ids, wts, x1, mod[l], moe_w1[l], moe_w3[l], moe_w2[l])
    return x
```

```python
import functools
import math

import numpy as np
import jax
import jax.numpy as jnp
from jax import lax
from jax.experimental import pallas as pl
from jax.experimental.pallas import tpu as pltpu

F32 = jnp.float32
BF16 = jnp.bfloat16

D_MODEL = 1024
HEAD_DIM = 64
HALF = HEAD_DIM // 2
A_HEADS = 4
B_HEADS = 8
C_HEADS = 16
SEC = 512
N_GROUPS = 4
N_EXPERTS = 8
N_EXP_TOTAL = N_GROUPS * N_EXPERTS
ROPE_THETA = 10000.0
RMS_EPS = 1e-6
QK_SCALE = HEAD_DIM ** -0.5
LOG2E = math.log2(math.e)

LANES = 128
TM = 512
TQ = 512
TK = 512
FOX_AUG = 16
FOX_K = HEAD_DIM + FOX_AUG
T_EXP = 256
TC = 256
NEG = -1e30
ROUTE_E0 = 8

NT_DIMS = (((1,), (1,)), ((), ()))
TN_DIMS = (((0,), (0,)), ((), ()))


def _dot(a, b):
    return jnp.dot(a, b, preferred_element_type=F32)


def _dot_nt(a, b):
    return lax.dot_general(a, b, NT_DIMS, preferred_element_type=F32)


def _dot_tn(a, b):
    return lax.dot_general(a, b, TN_DIMS, preferred_element_type=F32)


def _split2(x):
    hi = x.astype(BF16)
    lo = (x - hi.astype(F32)).astype(BF16)
    return hi, lo


def _split3(x):
    hi = x.astype(BF16)
    r = x - hi.astype(F32)
    mid = r.astype(BF16)
    lo = (r - mid.astype(F32)).astype(BF16)
    return hi, mid, lo


def _softplus_neg_abs(z):
    return jnp.log(1.0 + jnp.exp(-jnp.abs(z)))


def _log_sigmoid(z):
    return jnp.minimum(z, 0.0) - _softplus_neg_abs(z)


def _rms_rows(x, eps=RMS_EPS):
    return x * lax.rsqrt(jnp.mean(x * x, axis=-1, keepdims=True) + eps)


def _causal_pairs(n, descending):
    qi, ki = [], []
    for q in range(n):
        ks = range(q, -1, -1) if descending else range(q + 1)
        for k in ks:
            qi.append(q)
            ki.append(k)
    return np.asarray(qi, np.int32), np.asarray(ki, np.int32)


def _mod_kernel(c_ref, w_ref, b_ref, o_ref):
    c = c_ref[...]
    ca = c * jax.nn.sigmoid(c)
    c_hi, c_mid, c_lo = _split3(ca)
    w = w_ref[0]
    w_hi, w_lo = _split2(w)
    acc = _dot(c_hi, w_hi) + _dot(c_hi, w_lo) + _dot(c_mid, w_hi) + _dot(c_lo, w_hi) + _dot(c_mid, w_lo)
    o_ref[0] = acc + b_ref[0]


def _modulation(c, mod_w, mod_b):
    depth, d, n6 = mod_w.shape
    bsz = c.shape[0]
    rows = 8
    tn = 1536
    c_pad = jnp.zeros((rows, d), F32).at[:bsz].set(c)
    out = pl.pallas_call(
        _mod_kernel,
        grid=(depth, n6 // tn),
        in_specs=[
            pl.BlockSpec((rows, d), lambda l, j: (0, 0)),
            pl.BlockSpec((1, d, tn), lambda l, j: (l, 0, j)),
            pl.BlockSpec((1, 1, tn), lambda l, j: (l, 0, j)),
        ],
        out_specs=pl.BlockSpec((1, rows, tn), lambda l, j: (l, 0, j)),
        out_shape=jax.ShapeDtypeStruct((depth, rows, n6), F32),
        name="adaln_mod",
    )(c_pad, mod_w, mod_b.reshape(depth, 1, n6))
    return out[:, :bsz].reshape(depth, bsz, 6, d)


def _rope_table_kernel(pos_ref, inv_ref, cos_ref, sin_ref):
    ang = pos_ref[0].astype(F32) * inv_ref[...]
    cos_ref[0] = jnp.cos(ang)
    sin_ref[0] = jnp.sin(ang)


def _rope_tables(positions):
    bsz, seq = positions.shape
    ts = 2048
    inv = ROPE_THETA ** (-2.0 * jnp.arange(HALF, dtype=F32) / HEAD_DIM)
    return pl.pallas_call(
        _rope_table_kernel,
        grid=(bsz, seq // ts),
        in_specs=[
            pl.BlockSpec((1, 1, ts), lambda b, s: (b, 0, s)),
            pl.BlockSpec((HALF, 1), lambda b, s: (0, 0)),
        ],
        out_specs=[pl.BlockSpec((1, HALF, ts), lambda b, s: (b, 0, s))] * 2,
        out_shape=[jax.ShapeDtypeStruct((bsz, HALF, seq), F32)] * 2,
        name="rope_tables",
    )(positions.reshape(bsz, 1, seq), inv.reshape(HALF, 1))


def _adaln(x, mod_rows, g, first):
    shift = mod_rows[first:first + 1]
    scale = mod_rows[first + 1:first + 2]
    return _rms_rows(x) * g * (1.0 + scale) + shift


def _ln_proj_even_kernel(x_ref, mod_ref, g_ref, wT_ref, wfT_ref, bf_ref,
                         qna_ref, kna_ref, qnb_ref, knb_ref, cos_ref, sin_ref,
                         qa_o, ka_o, va_o, qb_o, kb_o, vb_o, carry):
    tm = x_ref.shape[1]

    @pl.when(pl.program_id(1) == 0)
    def _():
        carry[...] = jnp.zeros_like(carry)

    hb = _adaln(x_ref[0], mod_ref[0], g_ref[...], 0).astype(BF16)
    cos = cos_ref[0]
    sin = sin_ref[0]

    def section(idx):
        return _dot_nt(wT_ref[idx * SEC:(idx + 1) * SEC, :], hb)

    def norm_heads(p, g_col, out_ref, rope, scale, stride):
        for j in range(SEC // HEAD_DIM):
            xj = p[j * HEAD_DIM:(j + 1) * HEAD_DIM]
            yj = xj * lax.rsqrt(jnp.mean(xj * xj, axis=0, keepdims=True) + RMS_EPS) * g_col
            if rope:
                y1 = yj[:HALF]
                y2 = yj[HALF:]
                yj = jnp.concatenate([y1 * cos - y2 * sin, y2 * cos + y1 * sin], axis=0)
            out_ref[0, j * stride:j * stride + HEAD_DIM, :] = (yj * scale).astype(out_ref.dtype)

    softmax_q_scale = QK_SCALE * LOG2E
    norm_heads(section(0), qna_ref[...], qa_o, True, softmax_q_scale, HEAD_DIM)
    norm_heads(section(1), kna_ref[...], ka_o, True, 1.0, HEAD_DIM)
    va_o[0] = section(2).astype(va_o.dtype)
    norm_heads(section(3), qnb_ref[...], qb_o, False, softmax_q_scale, FOX_K)
    norm_heads(section(4), knb_ref[...], kb_o, False, 1.0, FOX_K)
    vb_o[0] = section(5).astype(vb_o.dtype)

    r = lax.broadcasted_iota(jnp.int32, (tm, tm), 0)
    c = lax.broadcasted_iota(jnp.int32, (tm, tm), 1)
    upto = jnp.where(r <= c, 1.0, 0.0).astype(BF16)
    log_f = _log_sigmoid(_dot_nt(wfT_ref[...], hb)[:B_HEADS] + bf_ref[...])
    f_hi, f_mid, f_lo = _split3(log_f)
    cum = _dot(f_hi, upto) + _dot(f_mid, upto) + _dot(f_lo, upto) + carry[...]
    carry[...] = cum[:, tm - 1:tm]

    row = lax.broadcasted_iota(jnp.int32, (FOX_AUG, tm), 0)
    for j in range(B_HEADS):
        pieces = [p.astype(F32) for p in _split3(cum[j:j + 1] * LOG2E)]
        q_aug = jnp.where(row < 3, 1.0, 0.0)
        k_aug = jnp.where((row >= 3) & (row < 6), 1.0, 0.0)
        for i, piece in enumerate(pieces):
            q_aug = jnp.where(row == 3 + i, piece, q_aug)
            k_aug = jnp.where(row == i, -piece, k_aug)
        lo = j * FOX_K + HEAD_DIM
        qb_o[0, lo:lo + FOX_AUG, :] = q_aug.astype(qb_o.dtype)
        kb_o[0, lo:lo + FOX_AUG, :] = k_aug.astype(kb_o.dtype)


def _ln_proj_even(x, mod, g, w_in, b_f, qn_a, kn_a, qn_b, kn_b, cos_t, sin_t):
    bsz, seq, d = x.shape
    n_main = 6 * SEC
    wT = w_in[:, :n_main].T.astype(BF16)
    wfT = jnp.zeros((16, d), F32).at[:B_HEADS].set(w_in[:, n_main:].T).astype(BF16)
    col = lambda v: v.reshape(HEAD_DIM, 1)
    full = lambda shape: pl.BlockSpec(shape, lambda b, s: (0,) * len(shape))
    rows_spec = lambda rows: pl.BlockSpec((1, rows, TM), lambda b, s: (b, 0, s))
    rows_shape = lambda rows: jax.ShapeDtypeStruct((bsz, rows, seq), BF16)
    out_rows = [SEC, SEC, SEC, B_HEADS * FOX_K, B_HEADS * FOX_K, SEC]
    return pl.pallas_call(
        _ln_proj_even_kernel,
        grid=(bsz, seq // TM),
        in_specs=[
            pl.BlockSpec((1, TM, d), lambda b, s: (b, s, 0)),
            pl.BlockSpec((1, 6, d), lambda b, s: (b, 0, 0)),
            full((1, d)), full((n_main, d)), full((16, d)), full((B_HEADS, 1)),
            full((HEAD_DIM, 1)), full((HEAD_DIM, 1)), full((HEAD_DIM, 1)), full((HEAD_DIM, 1)),
            rows_spec(HALF), rows_spec(HALF),
        ],
        out_specs=[rows_spec(n) for n in out_rows],
        out_shape=[rows_shape(n) for n in out_rows],
        scratch_shapes=[pltpu.VMEM((B_HEADS, 1), F32)],
        name="ln_proj_even",
    )(x, mod, g.reshape(1, d), wT, wfT, b_f.reshape(B_HEADS, 1), col(qn_a), col(kn_a), col(qn_b),
      col(kn_b), cos_t, sin_t)


def _ln_proj_odd_kernel(x_ref, mod_ref, g_ref, wT_ref, q_o, k_o, v_o):
    hb = _adaln(x_ref[0], mod_ref[0], g_ref[...], 0).astype(BF16)
    width = q_o.shape[1]
    for idx, (out_ref, scale) in enumerate(((q_o, QK_SCALE), (k_o, 1.0), (v_o, 1.0))):
        for half in range(2):
            lo = idx * width + half * (width // 2)
            p = _dot_nt(wT_ref[lo:lo + width // 2, :], hb)
            out_ref[0, half * (width // 2):(half + 1) * (width // 2), :] = (p * scale).astype(out_ref.dtype)


def _ln_proj_odd(x, mod, g, w_in):
    bsz, seq, d = x.shape
    width = C_HEADS * HEAD_DIM
    wT = w_in.T.astype(BF16)
    full = lambda shape: pl.BlockSpec(shape, lambda b, s: (0,) * len(shape))
    spec = pl.BlockSpec((1, width, TM), lambda b, s: (b, 0, s))
    shape = jax.ShapeDtypeStruct((bsz, width, seq), BF16)
    return pl.pallas_call(
        _ln_proj_odd_kernel,
        grid=(bsz, seq // TM),
        in_specs=[
            pl.BlockSpec((1, TM, d), lambda b, s: (b, s, 0)),
            pl.BlockSpec((1, 6, d), lambda b, s: (b, 0, 0)),
            full((1, d)), full((3 * width, d)),
        ],
        out_specs=[spec] * 3,
        out_shape=[shape] * 3,
        name="ln_proj_odd",
    )(x, mod, g.reshape(1, d), wT)


def _diag_valid(shape, strict):
    key = lax.broadcasted_iota(jnp.int32, shape, 0)
    qry = lax.broadcasted_iota(jnp.int32, shape, 1)
    return (qry > key) if strict else (qry >= key)


def _softmax_tile(kT, qT, vT, m_ref, l_ref, acc_ref, diag):
    sT = _dot_tn(kT, qT)
    if diag:
        sT = jnp.where(_diag_valid(sT.shape, False), sT, NEG)
    m_prev = m_ref[...]
    m_new = jnp.maximum(m_prev, jnp.max(sT, axis=0, keepdims=True))
    alpha = jnp.exp2(m_prev - m_new)
    p = jnp.exp2(sT - m_new)
    l_ref[...] = alpha * l_ref[...] + jnp.sum(p, axis=0, keepdims=True)
    acc_ref[...] = alpha * acc_ref[...] + _dot(vT, p.astype(BF16))
    m_ref[...] = m_new


def _init_softmax(m_refs, l_refs, acc_refs):
    for r in m_refs:
        r[...] = jnp.full_like(r, -jnp.inf)
    for r in l_refs + acc_refs:
        r[...] = jnp.zeros_like(r)


def _attn_diff_kernel(qi_ref, ki_ref, qT_ref, kT_ref, vT_ref, lam_ref, sg_ref, o_ref,
                      m0, m1, l0, l1, acc0, acc1, *, lambda_init):
    step = pl.program_id(2)
    qi = qi_ref[step]
    ki = ki_ref[step]
    ms, ls, accs = [m0, m1], [l0, l1], [acc0, acc1]

    @pl.when(ki == 0)
    def _():
        _init_softmax(ms, ls, accs)

    def tiles(diag):
        for c in range(2):
            hs = slice(c * HEAD_DIM, (c + 1) * HEAD_DIM)
            _softmax_tile(kT_ref[0, hs, :], qT_ref[0, hs, :], vT_ref[0], ms[c], ls[c], accs[c], diag)

    @pl.when(ki != qi)
    def _():
        tiles(False)

    @pl.when(ki == qi)
    def _():
        tiles(True)
        lam = lam_ref[...]
        e1 = jnp.exp(jnp.sum(lam[0:1] * lam[1:2], axis=-1, keepdims=True))
        e2 = jnp.exp(jnp.sum(lam[2:3] * lam[3:4], axis=-1, keepdims=True))
        lam_full = e1 - e2 + lambda_init
        oT = acc0[...] * (1.0 / l0[...]) - lam_full * (acc1[...] * (1.0 / l1[...]))
        oT = oT * lax.rsqrt(jnp.mean(oT * oT, axis=0, keepdims=True) + RMS_EPS)
        oT = oT * sg_ref[...] * (1.0 - lambda_init)
        o_ref[0] = oT.T.astype(o_ref.dtype)


def _attn_fox_kernel(qi_ref, ki_ref, qT_ref, kT_ref, vT_ref, o_ref, m0, m1, l0, l1, acc0, acc1):
    step = pl.program_id(2)
    qi = qi_ref[step]
    ki = ki_ref[step]
    ms, ls, accs = [m0, m1], [l0, l1], [acc0, acc1]

    @pl.when(ki == 0)
    def _():
        _init_softmax(ms, ls, accs)

    def tiles(diag):
        for j in range(2):
            ks = slice(j * FOX_K, (j + 1) * FOX_K)
            vs = slice(j * HEAD_DIM, (j + 1) * HEAD_DIM)
            _softmax_tile(kT_ref[0, ks, :], qT_ref[0, ks, :], vT_ref[0, vs, :], ms[j], ls[j], accs[j], diag)

    @pl.when(ki != qi)
    def _():
        tiles(False)

    @pl.when(ki == qi)
    def _():
        tiles(True)
        oT = jnp.concatenate([acc0[...] * (1.0 / l0[...]), acc1[...] * (1.0 / l1[...])], axis=0)
        o_ref[0] = oT.T.astype(o_ref.dtype)


def _sb_tile(kT, qT, vT, later_ref, r_ref, acc_ref, diag):
    zT = _dot_tn(kT, qT)
    sp = _softplus_neg_abs(zT)
    log_beta = jnp.minimum(zT, 0.0) - sp
    neg_keep = jnp.maximum(zT, 0.0) + sp
    if diag:
        valid = _diag_valid(zT.shape, True)
        neg_keep = jnp.where(valid, neg_keep, 0.0)
    laterT = _dot(later_ref[...], neg_keep.astype(BF16))
    r_prev = r_ref[...]
    a = jnp.exp(log_beta - laterT - r_prev)
    if diag:
        a = jnp.where(valid, a, 0.0)
    acc_ref[...] = acc_ref[...] + _dot(vT, a.astype(BF16))
    r_ref[...] = r_prev + laterT[0:1] + neg_keep[0:1]


def _attn_sb_kernel(qi_ref, ki_ref, qT_ref, kT_ref, vT_ref, later_ref, o_ref, r0, r1, acc0, acc1):
    step = pl.program_id(2)
    qi = qi_ref[step]
    ki = ki_ref[step]
    rs, accs = [r0, r1], [acc0, acc1]

    def tiles(diag):
        for j in range(2):
            hs = slice(j * HEAD_DIM, (j + 1) * HEAD_DIM)
            _sb_tile(kT_ref[0, hs, :], qT_ref[0, hs, :], vT_ref[0, hs, :], later_ref, rs[j], accs[j], diag)

    @pl.when(ki == qi)
    def _():
        for r in rs + accs:
            r[...] = jnp.zeros_like(r)
        tiles(True)

    @pl.when(ki != qi)
    def _():
        tiles(False)

    @pl.when(ki == 0)
    def _():
        o_ref[0] = jnp.concatenate([acc0[...], acc1[...]], axis=0).T.astype(o_ref.dtype)


def _attention_call(body, name, n_groups, qT, kT, vT, qk_rows, v_rows, extra_in, extra_specs, scratch,
                    descending):
    bsz, _, seq = qT.shape
    qi_tab, ki_tab = _causal_pairs(seq // TQ, descending)
    in_specs = [
        pl.BlockSpec((1, qk_rows, TQ), lambda b, h, p, qi, ki: (b, h, qi[p])),
        pl.BlockSpec((1, qk_rows, TK), lambda b, h, p, qi, ki: (b, h, ki[p])),
        pl.BlockSpec((1, v_rows, TK), lambda b, h, p, qi, ki: (b, h, ki[p])),
    ] + extra_specs
    grid_spec = pltpu.PrefetchScalarGridSpec(
        num_scalar_prefetch=2,
        grid=(bsz, n_groups, len(qi_tab)),
        in_specs=in_specs,
        out_specs=pl.BlockSpec((1, TQ, 2 * HEAD_DIM), lambda b, h, p, qi, ki: (b, qi[p], h)),
        scratch_shapes=scratch,
    )
    return pl.pallas_call(
        body,
        grid_spec=grid_spec,
        out_shape=jax.ShapeDtypeStruct((bsz, seq, n_groups * 2 * HEAD_DIM), BF16),
        name=name,
    )(jnp.asarray(qi_tab), jnp.asarray(ki_tab), qT, kT, vT, *extra_in)


def _stat_scratch(n_stats, v_dim):
    return [pltpu.VMEM((1, TQ), F32)] * n_stats + [pltpu.VMEM((v_dim, TQ), F32)] * 2


def _even_mixer_attention(qa, ka, va, qb, kb, vb, lam, subln_g, lambda_init):
    full = lambda shape: pl.BlockSpec(shape, lambda b, h, p, qi, ki: (0,) * len(shape))
    pair = 2 * HEAD_DIM
    o_a = _attention_call(
        functools.partial(_attn_diff_kernel, lambda_init=lambda_init), "attn_diff", A_HEADS,
        qa, ka, va, pair, pair, [lam, subln_g.reshape(pair, 1)],
        [full((4, HEAD_DIM)), full((pair, 1))], _stat_scratch(4, pair), False)
    o_b = _attention_call(_attn_fox_kernel, "attn_fox", B_HEADS // 2, qb, kb, vb, 2 * FOX_K, pair,
                          [], [], _stat_scratch(4, HEAD_DIM), False)
    return [o_a, o_b]


def _sb_attention(qT, kT, vT):
    full = lambda shape: pl.BlockSpec(shape, lambda b, h, p, qi, ki: (0,) * len(shape))
    pair = 2 * HEAD_DIM
    later = jnp.triu(jnp.ones((TK, TK), F32), 1).astype(BF16)
    return [_attention_call(_attn_sb_kernel, "attn_sb", C_HEADS // 2, qT, kT, vT, pair, pair,
                            [later], [full((TK, TK))], _stat_scratch(2, HEAD_DIM), True)]


def _out_proj_route_kernel(*refs, n_parts):
    o_refs = refs[:n_parts]
    wo_refs = refs[n_parts:2 * n_parts]
    x_ref, mod_ref, g_ref, wrh_ref, wrl_ref, br_ref, x1_o, h2_o, ids_o, wts_o = refs[2 * n_parts:]
    mod_rows = mod_ref[0]
    gate1 = mod_rows[2:3]
    mix = _dot(o_refs[0][0], wo_refs[0][...])
    for o_ref, wo_ref in zip(o_refs[1:], wo_refs[1:]):
        mix = mix + _dot(o_ref[0], wo_ref[...])
    x1 = x_ref[0] + gate1 * mix
    x1_o[0] = x1
    h2 = _adaln(x1, mod_rows, g_ref[...], 3)
    h2_o[0] = h2

    h_hi, h_lo = _split2(h2)
    wr_hi = wrh_ref[...]
    logits = _dot(h_hi, wr_hi) + _dot(h_hi, wrl_ref[...]) + _dot(h_lo, wr_hi) + br_ref[...]

    lane = lax.broadcasted_iota(jnp.int32, logits.shape, 1)
    big = jnp.int32(LANES)
    g_mask = lane < N_GROUPS
    g_log = jnp.where(g_mask, logits, NEG)
    g_max = jnp.max(g_log, axis=-1, keepdims=True)
    g_sum = jnp.sum(jnp.where(g_mask, jnp.exp(g_log - g_max), 0.0), axis=-1, keepdims=True)
    g_w = 1.0 / g_sum
    g_idx = jnp.min(jnp.where(g_mask & (g_log == g_max), lane, big), axis=-1, keepdims=True)

    e_mask = (lane >= ROUTE_E0) & (lane < ROUTE_E0 + N_EXP_TOTAL) & ((lane >> 3) == g_idx + 1)
    e_log = jnp.where(e_mask, logits, NEG)
    e_max = jnp.max(e_log, axis=-1, keepdims=True)
    e_exp = jnp.where(e_mask, jnp.exp(e_log - e_max), 0.0)
    e_prob = e_exp / jnp.sum(e_exp, axis=-1, keepdims=True)
    p1 = jnp.max(jnp.where(e_mask, e_prob, -1.0), axis=-1, keepdims=True)
    i1 = jnp.min(jnp.where(e_mask & (e_prob == p1), lane, big), axis=-1, keepdims=True)
    rest = e_mask & (lane != i1)
    p2 = jnp.max(jnp.where(rest, e_prob, -1.0), axis=-1, keepdims=True)
    i2 = jnp.min(jnp.where(rest & (e_prob == p2), lane, big), axis=-1, keepdims=True)
    den = p1 + p2
    w1 = p1 / den * g_w
    w2 = p2 / den * g_w
    ids_o[0] = jnp.where(lane == 0, i1 - ROUTE_E0, jnp.where(lane == 1, i2 - ROUTE_E0, 0))
    wts_o[0] = jnp.where(lane == 0, w1, jnp.where(lane == 1, w2, 0.0))


def _out_proj_route(o_parts, x, mod, g2, w_out, w_gr, b_gr, w_er, b_er):
    bsz, seq, d = x.shape
    widths = [o.shape[-1] for o in o_parts]
    starts = np.cumsum([0] + widths)
    wo_parts = [w_out[starts[i]:starts[i + 1]].astype(BF16) for i in range(len(o_parts))]
    wr = jnp.zeros((d, LANES), F32)
    wr = wr.at[:, :N_GROUPS].set(w_gr)
    wr = wr.at[:, ROUTE_E0:ROUTE_E0 + N_EXP_TOTAL].set(
        jnp.transpose(w_er, (1, 0, 2)).reshape(d, N_EXP_TOTAL))
    br = jnp.zeros((1, LANES), F32)
    br = br.at[0, :N_GROUPS].set(b_gr)
    br = br.at[0, ROUTE_E0:ROUTE_E0 + N_EXP_TOTAL].set(b_er.reshape(N_EXP_TOTAL))
    wr_hi = wr.astype(BF16)
    wr_lo = (wr - wr_hi.astype(F32)).astype(BF16)
    full = lambda shape: pl.BlockSpec(shape, lambda b, s: (0,) * len(shape))
    tile = lambda w: pl.BlockSpec((1, TM, w), lambda b, s: (b, s, 0))
    return pl.pallas_call(
        functools.partial(_out_proj_route_kernel, n_parts=len(o_parts)),
        grid=(bsz, seq // TM),
        in_specs=[tile(w) for w in widths] + [full((w, d)) for w in widths] + [
            tile(d),
            pl.BlockSpec((1, 6, d), lambda b, s: (b, 0, 0)),
            full((1, d)), full((d, LANES)), full((d, LANES)), full((1, LANES)),
        ],
        out_specs=[tile(d), tile(d), tile(LANES), tile(LANES)],
        out_shape=[
            jax.ShapeDtypeStruct((bsz, seq, d), F32),
            jax.ShapeDtypeStruct((bsz, seq, d), F32),
            jax.ShapeDtypeStruct((bsz, seq, LANES), jnp.int32),
            jax.ShapeDtypeStruct((bsz, seq, LANES), F32),
        ],
        name="out_proj_route",
    )(*o_parts, *wo_parts, x, mod, g2.reshape(1, d), wr_hi, wr_lo, br)


def _expert_onehots(ids_ref):
    tc = ids_ref.shape[2]
    expert = lax.broadcasted_iota(jnp.int32, (N_EXP_TOTAL, tc), 0)
    ids = ids_ref[0]
    return [jnp.where(expert == ids[k:k + 1], 1.0, 0.0) for k in range(2)]


def _rank_kernel(ids_ref, rank_o, cnt_o, carry):
    tc = ids_ref.shape[2]

    @pl.when(pl.program_id(0) == 0)
    def _():
        carry[...] = jnp.zeros_like(carry)

    r = lax.broadcasted_iota(jnp.int32, (tc, tc), 0)
    c = lax.broadcasted_iota(jnp.int32, (tc, tc), 1)
    before = jnp.where(r < c, 1.0, 0.0).astype(BF16)
    base = carry[...]
    rank_o[0] = jnp.zeros(rank_o.shape[1:], rank_o.dtype)
    for k, oh in enumerate(_expert_onehots(ids_ref)):
        prefix = _dot(oh.astype(BF16), before)
        rank_o[0, k:k + 1, :] = jnp.sum(oh * (base + prefix), axis=0, keepdims=True).astype(jnp.int32)
        base = base + jnp.sum(oh, axis=-1, keepdims=True)
    carry[...] = base
    cnt_o[...] = jnp.broadcast_to(base, cnt_o.shape)


def _pos_kernel(ids_ref, rank_ref, off_ref, pos_o):
    rank = rank_ref[0]
    pos_o[0] = jnp.zeros(pos_o.shape[1:], pos_o.dtype)
    for k, oh in enumerate(_expert_onehots(ids_ref)):
        off = jnp.sum(oh * off_ref[...], axis=0, keepdims=True).astype(jnp.int32)
        pos_o[0, k:k + 1, :] = off + rank[k:k + 1]


def _dispatch_plan(ids2):
    n = ids2.shape[0]
    nc = n // TC
    ids_t = jnp.zeros((nc, 8, TC), jnp.int32).at[:, :2].set(
        jnp.transpose(ids2.reshape(nc, TC, 2), (0, 2, 1)))
    blk = pl.BlockSpec((1, 8, TC), lambda c: (c, 0, 0))
    rank, cnt = pl.pallas_call(
        _rank_kernel,
        grid=(nc,),
        in_specs=[blk],
        out_specs=[blk, pl.BlockSpec((N_EXP_TOTAL, LANES), lambda c: (0, 0))],
        out_shape=[jax.ShapeDtypeStruct((nc, 8, TC), jnp.int32),
                   jax.ShapeDtypeStruct((N_EXP_TOTAL, LANES), F32)],
        scratch_shapes=[pltpu.VMEM((N_EXP_TOTAL, 1), F32)],
        name="moe_rank",
    )(ids_t)
    counts = cnt[:, 0].astype(jnp.int32)
    tiles_per = (counts + T_EXP - 1) // T_EXP
    tile_start = jnp.cumsum(tiles_per) - tiles_per
    n_used = jnp.sum(tiles_per)
    pos = pl.pallas_call(
        _pos_kernel,
        grid=(nc,),
        in_specs=[blk, blk, pl.BlockSpec((N_EXP_TOTAL, 1), lambda c: (0, 0))],
        out_specs=blk,
        out_shape=jax.ShapeDtypeStruct((nc, 8, TC), jnp.int32),
        name="moe_pos",
    )(ids_t, rank, (tile_start * T_EXP).astype(F32).reshape(N_EXP_TOTAL, 1))
    n_tiles = (2 * n) // T_EXP + N_EXP_TOTAL
    tile_idx = jnp.minimum(jnp.arange(n_tiles, dtype=jnp.int32), n_used - 1)
    tile_expert = jnp.sum((tile_start[None, :] <= tile_idx[:, None]).astype(jnp.int32), axis=1) - 1
    pos_chunks = pos[:, :2].reshape(nc, 1, 2 * TC)
    return pos_chunks, tile_expert, n_used.reshape(1).astype(jnp.int32), n_tiles


def _row_copy_wait(src, dst, sem, rows):
    pltpu.make_async_copy(src.at[pl.ds(0, rows)], dst.at[pl.ds(0, rows)], sem).wait()


def _dispatch_kernel(pos_ref, h_ref, xs_in_hbm, xs_hbm, sem):
    del xs_in_hbm

    def issue(i, carry):
        for k in range(2):
            dst = pos_ref[0, 0, k * TC + i]
            pltpu.make_async_copy(h_ref.at[pl.ds(i, 1)], xs_hbm.at[pl.ds(dst, 1)], sem).start()
        return carry

    lax.fori_loop(0, TC, issue, 0)
    for k in range(2):
        _row_copy_wait(h_ref, xs_hbm, sem, TC)


def _dispatch(h2_flat, pos_chunks, n_rows):
    n, d = h2_flat.shape
    xs0 = jnp.zeros((n_rows, d), h2_flat.dtype)
    return pl.pallas_call(
        _dispatch_kernel,
        grid=(n // TC,),
        in_specs=[
            pl.BlockSpec((1, 1, 2 * TC), lambda c: (c, 0, 0), memory_space=pltpu.SMEM),
            pl.BlockSpec((TC, d), lambda c: (c, 0)),
            pl.BlockSpec(memory_space=pl.ANY),
        ],
        out_specs=pl.BlockSpec(memory_space=pl.ANY),
        out_shape=jax.ShapeDtypeStruct((n_rows, d), h2_flat.dtype),
        scratch_shapes=[pltpu.SemaphoreType.DMA(())],
        input_output_aliases={2: 0},
        name="moe_dispatch",
    )(pos_chunks, h2_flat, xs0)


def _expert_kernel(te_ref, nu_ref, xs_ref, w1_ref, w3_ref, w2_ref, ys_ref, w1_s, w3_s, w2_s):
    j = pl.program_id(0)
    prev = te_ref[jnp.maximum(j - 1, 0)]

    @pl.when((j == 0) | (te_ref[j] != prev))
    def _():
        w1_s[...] = w1_ref[0].astype(BF16)
        w3_s[...] = w3_ref[0].astype(BF16)
        w2_s[...] = w2_ref[0].astype(BF16)

    @pl.when(j < nu_ref[0])
    def _():
        xb = xs_ref[...].astype(BF16)
        a = _dot(xb, w1_s[...])
        b = _dot(xb, w3_s[...])
        hid = a * jax.nn.sigmoid(a) * b
        ys_ref[...] = _dot(hid.astype(BF16), w2_s[...])

    @pl.when(j >= nu_ref[0])
    def _():
        ys_ref[...] = jnp.zeros_like(ys_ref)


def _experts(xs, tile_expert, n_used, w1, w3, w2):
    n_rows, d = xs.shape
    f = w1.shape[-1]
    grid_spec = pltpu.PrefetchScalarGridSpec(
        num_scalar_prefetch=2,
        grid=(n_rows // T_EXP,),
        in_specs=[
            pl.BlockSpec((T_EXP, d), lambda j, te, nu: (j, 0)),
            pl.BlockSpec((1, d, f), lambda j, te, nu: (te[j], 0, 0)),
            pl.BlockSpec((1, d, f), lambda j, te, nu: (te[j], 0, 0)),
            pl.BlockSpec((1, f, d), lambda j, te, nu: (te[j], 0, 0)),
        ],
        out_specs=pl.BlockSpec((T_EXP, d), lambda j, te, nu: (j, 0)),
        scratch_shapes=[pltpu.VMEM((d, f), BF16), pltpu.VMEM((d, f), BF16), pltpu.VMEM((f, d), BF16)],
    )
    return pl.pallas_call(
        _expert_kernel,
        grid_spec=grid_spec,
        out_shape=jax.ShapeDtypeStruct((n_rows, d), F32),
        name="moe_experts",
    )(tile_expert, n_used, xs, w1.reshape(N_EXP_TOTAL, d, f), w3.reshape(N_EXP_TOTAL, d, f),
      w2.reshape(N_EXP_TOTAL, f, d))


def _combine_kernel(pos_ref, ys_hbm, x1_ref, wts_ref, mod_ref, x2_o, buf, sem):
    def issue(i, carry):
        for k in range(2):
            src = pos_ref[0, 0, k * TC + i]
            pltpu.make_async_copy(ys_hbm.at[pl.ds(src, 1)], buf.at[k, pl.ds(i, 1)], sem).start()
        return carry

    lax.fori_loop(0, TC, issue, 0)
    for k in range(2):
        _row_copy_wait(ys_hbm, buf.at[k], sem, TC)
    wts = wts_ref[0]
    y = wts[:, 0:1] * buf[0] + wts[:, 1:2] * buf[1]
    gate2 = mod_ref[0][5:6]
    x2_o[0] = x1_ref[0] + gate2 * y


def _combine(ys, pos_chunks, x1, wts, mod):
    bsz, seq, d = x1.shape
    per_b = seq // TC
    return pl.pallas_call(
        _combine_kernel,
        grid=(bsz, per_b),
        in_specs=[
            pl.BlockSpec((1, 1, 2 * TC), lambda b, s: (b * per_b + s, 0, 0), memory_space=pltpu.SMEM),
            pl.BlockSpec(memory_space=pl.ANY),
            pl.BlockSpec((1, TC, d), lambda b, s: (b, s, 0)),
            pl.BlockSpec((1, TC, LANES), lambda b, s: (b, s, 0)),
            pl.BlockSpec((1, 6, d), lambda b, s: (b, 0, 0)),
        ],
        out_specs=pl.BlockSpec((1, TC, d), lambda b, s: (b, s, 0)),
        out_shape=jax.ShapeDtypeStruct((bsz, seq, d), F32),
        scratch_shapes=[pltpu.VMEM((2, TC, d), F32), pltpu.SemaphoreType.DMA(())],
        name="moe_combine",
    )(pos_chunks, ys, x1, wts, mod)


def _moe(h2, ids, wts, x1, mod, w1, w3, w2):
    bsz, seq, d = h2.shape
    n = bsz * seq
    pos_chunks, tile_expert, n_used, n_tiles = _dispatch_plan(ids.reshape(n, LANES)[:, :2])
    xs = _dispatch(h2.reshape(n, d), pos_chunks, n_tiles * T_EXP)
    ys = _experts(xs, tile_expert, n_used, w1, w3, w2)
    return _combine(ys, pos_chunks, x1, wts, mod)


def kernel(x, c, positions, mod_w, mod_b, norm1_g, norm2_g, ev_w_in, ev_b_f, ev_qn_a, ev_kn_a, ev_lam,
           ev_subln_g, ev_qn_b, ev_kn_b, ev_w_out, od_w_in, od_w_out, moe_w_gr, moe_b_gr, moe_w_er,
           moe_b_er, moe_w1, moe_w3, moe_w2):
    depth = mod_w.shape[0]
    mod = _modulation(c, mod_w, mod_b)
    cos_t, sin_t = _rope_tables(positions)
    for l in range(depth):
        if l % 2 == 0:
            e = l // 2
            lambda_init = 0.8 - 0.6 * math.exp(-0.3 * l)
            qa, ka, va, qb, kb, vb = _ln_proj_even(
                x, mod[l], norm1_g[l], ev_w_in[e], ev_b_f[e], ev_qn_a[e], ev_kn_a[e], ev_qn_b[e],
                ev_kn_b[e], cos_t, sin_t)
            o = _even_mixer_attention(qa, ka, va, qb, kb, vb, ev_lam[e], ev_subln_g[e], lambda_init)
            w_out = ev_w_out[e]
        else:
            od = l // 2
            q, k, v = _ln_proj_odd(x, mod[l], norm1_g[l], od_w_in[od])
            o = _sb_attention(q, k, v)
            w_out = od_w_out[od]
        x1, h2, ids, wts = _out_proj_route(o, x, mod[l], norm2_g[l], w_out, moe_w_gr[l], moe_b_gr[l],
                                           moe_w_er[l], moe_b_er[l])
        x = _moe(h2, ids, wts, x1, mod[l], moe_w1[l], moe_w3[l], moe_w2[l])
    return x
```

```python
import functools
import math

import numpy as np
import jax
import jax.numpy as jnp
from jax import lax
from jax.experimental import pallas as pl
from jax.experimental.pallas import tpu as pltpu

F32 = jnp.float32
BF16 = jnp.bfloat16

D_MODEL = 1024
HEAD_DIM = 64
HALF = HEAD_DIM // 2
A_HEADS = 4
B_HEADS = 8
C_HEADS = 16
SEC = 512
N_GROUPS = 4
N_EXPERTS = 8
N_EXP_TOTAL = N_GROUPS * N_EXPERTS
ROPE_THETA = 10000.0
RMS_EPS = 1e-6
QK_SCALE = HEAD_DIM ** -0.5
LOG2E = math.log2(math.e)

LANES = 128
TM = 512
TQ = 1024
TK_SOFTMAX = 1024
TK_STICK = 512
FOX_AUG = 16
FOX_K = HEAD_DIM + FOX_AUG
T_EXP = 256
TC = 256
ISSUE_UNROLL = 8
NEG = -1e30
ROUTE_E0 = 8

NT_DIMS = (((1,), (1,)), ((), ()))
TN_DIMS = (((0,), (0,)), ((), ()))


def _dot(a, b):
    return jnp.dot(a, b, preferred_element_type=F32)


def _dot_nt(a, b):
    return lax.dot_general(a, b, NT_DIMS, preferred_element_type=F32)


def _dot_tn(a, b):
    return lax.dot_general(a, b, TN_DIMS, preferred_element_type=F32)


def _split2(x):
    hi = x.astype(BF16)
    lo = (x - hi.astype(F32)).astype(BF16)
    return hi, lo


def _split3(x):
    hi = x.astype(BF16)
    r = x - hi.astype(F32)
    mid = r.astype(BF16)
    lo = (r - mid.astype(F32)).astype(BF16)
    return hi, mid, lo


def _softplus_neg_abs(z):
    return jnp.log(1.0 + jnp.exp(-jnp.abs(z)))


def _log_sigmoid(z):
    return jnp.minimum(z, 0.0) - _softplus_neg_abs(z)


def _rms_rows(x, eps=RMS_EPS):
    return x * lax.rsqrt(jnp.mean(x * x, axis=-1, keepdims=True) + eps)


def _mod_kernel(c_ref, w_ref, b_ref, o_ref):
    c = c_ref[...]
    ca = c * jax.nn.sigmoid(c)
    c_hi, c_mid, c_lo = _split3(ca)
    w = w_ref[0]
    w_hi, w_lo = _split2(w)
    acc = _dot(c_hi, w_hi) + _dot(c_hi, w_lo) + _dot(c_mid, w_hi) + _dot(c_lo, w_hi) + _dot(c_mid, w_lo)
    o_ref[0] = acc + b_ref[0]


def _modulation(c, mod_w, mod_b):
    depth, d, n6 = mod_w.shape
    bsz = c.shape[0]
    rows = 8
    tn = 1536
    c_pad = jnp.zeros((rows, d), F32).at[:bsz].set(c)
    out = pl.pallas_call(
        _mod_kernel,
        grid=(depth, n6 // tn),
        in_specs=[
            pl.BlockSpec((rows, d), lambda l, j: (0, 0)),
            pl.BlockSpec((1, d, tn), lambda l, j: (l, 0, j)),
            pl.BlockSpec((1, 1, tn), lambda l, j: (l, 0, j)),
        ],
        out_specs=pl.BlockSpec((1, rows, tn), lambda l, j: (l, 0, j)),
        out_shape=jax.ShapeDtypeStruct((depth, rows, n6), F32),
        name="adaln_mod",
    )(c_pad, mod_w, mod_b.reshape(depth, 1, n6))
    return out[:, :bsz].reshape(depth, bsz, 6, d)


def _rope_table_kernel(pos_ref, inv_ref, cos_ref, sin_ref):
    ang = pos_ref[0].astype(F32) * inv_ref[...]
    cos_ref[0] = jnp.cos(ang)
    sin_ref[0] = jnp.sin(ang)


def _rope_tables(positions):
    bsz, seq = positions.shape
    ts = 2048
    inv = ROPE_THETA ** (-2.0 * jnp.arange(HALF, dtype=F32) / HEAD_DIM)
    return pl.pallas_call(
        _rope_table_kernel,
        grid=(bsz, seq // ts),
        in_specs=[
            pl.BlockSpec((1, 1, ts), lambda b, s: (b, 0, s)),
            pl.BlockSpec((HALF, 1), lambda b, s: (0, 0)),
        ],
        out_specs=[pl.BlockSpec((1, HALF, ts), lambda b, s: (b, 0, s))] * 2,
        out_shape=[jax.ShapeDtypeStruct((bsz, HALF, seq), F32)] * 2,
        name="rope_tables",
    )(positions.reshape(bsz, 1, seq), inv.reshape(HALF, 1))


def _adaln(x, mod_rows, g, first):
    shift = mod_rows[first:first + 1]
    scale = mod_rows[first + 1:first + 2]
    return _rms_rows(x) * g * (1.0 + scale) + shift


def _ln_proj_even_kernel(x_ref, mod_ref, g_ref, wT_ref, wfT_ref, bf_ref,
                         qna_ref, kna_ref, qnb_ref, knb_ref, cos_ref, sin_ref,
                         qa_o, ka_o, va_o, qb_o, kb_o, vb_o, carry):
    tm = x_ref.shape[1]

    @pl.when(pl.program_id(1) == 0)
    def _():
        carry[...] = jnp.zeros_like(carry)

    hb = _adaln(x_ref[0], mod_ref[0], g_ref[...], 0).astype(BF16)
    cos = cos_ref[0]
    sin = sin_ref[0]

    def section(idx):
        return _dot_nt(wT_ref[idx * SEC:(idx + 1) * SEC, :], hb)

    def norm_heads(p, g_col, out_ref, rope, scale, stride):
        for j in range(SEC // HEAD_DIM):
            xj = p[j * HEAD_DIM:(j + 1) * HEAD_DIM]
            yj = xj * lax.rsqrt(jnp.mean(xj * xj, axis=0, keepdims=True) + RMS_EPS) * g_col
            if rope:
                y1 = yj[:HALF]
                y2 = yj[HALF:]
                yj = jnp.concatenate([y1 * cos - y2 * sin, y2 * cos + y1 * sin], axis=0)
            out_ref[0, j * stride:j * stride + HEAD_DIM, :] = (yj * scale).astype(out_ref.dtype)

    softmax_q_scale = QK_SCALE * LOG2E
    norm_heads(section(0), qna_ref[...], qa_o, True, softmax_q_scale, HEAD_DIM)
    norm_heads(section(1), kna_ref[...], ka_o, True, 1.0, HEAD_DIM)
    va_o[0] = section(2).astype(va_o.dtype)
    norm_heads(section(3), qnb_ref[...], qb_o, False, softmax_q_scale, FOX_K)
    norm_heads(section(4), knb_ref[...], kb_o, False, 1.0, FOX_K)
    vb_o[0] = section(5).astype(vb_o.dtype)

    r = lax.broadcasted_iota(jnp.int32, (tm, tm), 0)
    c = lax.broadcasted_iota(jnp.int32, (tm, tm), 1)
    upto = jnp.where(r <= c, 1.0, 0.0).astype(BF16)
    log_f = _log_sigmoid(_dot_nt(wfT_ref[...], hb)[:B_HEADS] + bf_ref[...])
    f_hi, f_mid, f_lo = _split3(log_f)
    cum = _dot(f_hi, upto) + _dot(f_mid, upto) + _dot(f_lo, upto) + carry[...]
    carry[...] = cum[:, tm - 1:tm]

    row = lax.broadcasted_iota(jnp.int32, (FOX_AUG, tm), 0)
    for j in range(B_HEADS):
        pieces = [p.astype(F32) for p in _split3(cum[j:j + 1] * LOG2E)]
        q_aug = jnp.where(row < 3, 1.0, 0.0)
        k_aug = jnp.where((row >= 3) & (row < 6), 1.0, 0.0)
        for i, piece in enumerate(pieces):
            q_aug = jnp.where(row == 3 + i, piece, q_aug)
            k_aug = jnp.where(row == i, -piece, k_aug)
        lo = j * FOX_K + HEAD_DIM
        qb_o[0, lo:lo + FOX_AUG, :] = q_aug.astype(qb_o.dtype)
        kb_o[0, lo:lo + FOX_AUG, :] = k_aug.astype(kb_o.dtype)


def _ln_proj_even(x, mod, g, w_in, b_f, qn_a, kn_a, qn_b, kn_b, cos_t, sin_t):
    bsz, seq, d = x.shape
    n_main = 6 * SEC
    wT = w_in[:, :n_main].T.astype(BF16)
    wfT = jnp.zeros((16, d), F32).at[:B_HEADS].set(w_in[:, n_main:].T).astype(BF16)
    col = lambda v: v.reshape(HEAD_DIM, 1)
    full = lambda shape: pl.BlockSpec(shape, lambda b, s: (0,) * len(shape))
    rows_spec = lambda rows: pl.BlockSpec((1, rows, TM), lambda b, s: (b, 0, s))
    rows_shape = lambda rows: jax.ShapeDtypeStruct((bsz, rows, seq), BF16)
    out_rows = [SEC, SEC, SEC, B_HEADS * FOX_K, B_HEADS * FOX_K, SEC]
    return pl.pallas_call(
        _ln_proj_even_kernel,
        grid=(bsz, seq // TM),
        in_specs=[
            pl.BlockSpec((1, TM, d), lambda b, s: (b, s, 0)),
            pl.BlockSpec((1, 6, d), lambda b, s: (b, 0, 0)),
            full((1, d)), full((n_main, d)), full((16, d)), full((B_HEADS, 1)),
            full((HEAD_DIM, 1)), full((HEAD_DIM, 1)), full((HEAD_DIM, 1)), full((HEAD_DIM, 1)),
            rows_spec(HALF), rows_spec(HALF),
        ],
        out_specs=[rows_spec(n) for n in out_rows],
        out_shape=[rows_shape(n) for n in out_rows],
        scratch_shapes=[pltpu.VMEM((B_HEADS, 1), F32)],
        name="ln_proj_even",
    )(x, mod, g.reshape(1, d), wT, wfT, b_f.reshape(B_HEADS, 1), col(qn_a), col(kn_a), col(qn_b),
      col(kn_b), cos_t, sin_t)


def _ln_proj_odd_kernel(x_ref, mod_ref, g_ref, wT_ref, q_o, k_o, v_o):
    hb = _adaln(x_ref[0], mod_ref[0], g_ref[...], 0).astype(BF16)
    width = q_o.shape[1]
    for idx, (out_ref, scale) in enumerate(((q_o, QK_SCALE * LOG2E), (k_o, 1.0), (v_o, 1.0))):
        for half in range(2):
            lo = idx * width + half * (width // 2)
            p = _dot_nt(wT_ref[lo:lo + width // 2, :], hb)
            out_ref[0, half * (width // 2):(half + 1) * (width // 2), :] = (p * scale).astype(out_ref.dtype)


def _ln_proj_odd(x, mod, g, w_in):
    bsz, seq, d = x.shape
    width = C_HEADS * HEAD_DIM
    wT = w_in.T.astype(BF16)
    full = lambda shape: pl.BlockSpec(shape, lambda b, s: (0,) * len(shape))
    spec = pl.BlockSpec((1, width, TM), lambda b, s: (b, 0, s))
    shape = jax.ShapeDtypeStruct((bsz, width, seq), BF16)
    return pl.pallas_call(
        _ln_proj_odd_kernel,
        grid=(bsz, seq // TM),
        in_specs=[
            pl.BlockSpec((1, TM, d), lambda b, s: (b, s, 0)),
            pl.BlockSpec((1, 6, d), lambda b, s: (b, 0, 0)),
            full((1, d)), full((3 * width, d)),
        ],
        out_specs=[spec] * 3,
        out_shape=[shape] * 3,
        name="ln_proj_odd",
    )(x, mod, g.reshape(1, d), wT)


def _diag_valid(shape, strict):
    key = lax.broadcasted_iota(jnp.int32, shape, 0)
    qry = lax.broadcasted_iota(jnp.int32, shape, 1)
    return (qry > key) if strict else (qry >= key)


def _query_cols(diag):
    return slice(0 if diag is None else diag, TQ)


def _softmax_tile(kT, qT_ref, q_rows, vT, m_ref, l_ref, acc_ref, diag):
    qs = _query_cols(diag)
    sT = _dot_tn(kT, qT_ref[0, q_rows, qs])
    if diag is not None:
        sT = jnp.where(_diag_valid(sT.shape, False), sT, NEG)
    m_prev = m_ref[:, qs]
    m_new = jnp.maximum(m_prev, jnp.max(sT, axis=0, keepdims=True))
    alpha = jnp.exp2(m_prev - m_new)
    p = jnp.exp2(sT - m_new)
    l_ref[:, qs] = alpha * l_ref[:, qs] + jnp.sum(p, axis=0, keepdims=True)
    acc_ref[:, qs] = alpha * acc_ref[:, qs] + _dot(vT, p.astype(BF16))
    m_ref[:, qs] = m_new


def _init_softmax(m_refs, l_refs, acc_refs):
    for r in m_refs:
        r[...] = jnp.full_like(r, -jnp.inf)
    for r in l_refs + acc_refs:
        r[...] = jnp.zeros_like(r)


def _key_tile(ki, tk):
    return pl.ds(pl.multiple_of(ki * tk, tk), tk)


def _attn_diff_kernel(qT_ref, kT_ref, vT_ref, lam_ref, sg_ref, o_ref,
                      m0, m1, l0, l1, acc0, acc1, *, lambda_init):
    qi = pl.program_id(2)
    ms, ls, accs = [m0, m1], [l0, l1], [acc0, acc1]
    _init_softmax(ms, ls, accs)

    def tiles(ki, diag):
        cols = _key_tile(ki, TK_SOFTMAX)
        for c in range(2):
            hs = slice(c * HEAD_DIM, (c + 1) * HEAD_DIM)
            _softmax_tile(kT_ref[0, hs, cols], qT_ref, hs, vT_ref[0, :, cols], ms[c], ls[c], accs[c], diag)

    def below_diagonal(ki, carry):
        tiles(ki, None)
        return carry

    diag_tiles = TQ // TK_SOFTMAX
    n_below = qi * diag_tiles
    lax.fori_loop(0, n_below, below_diagonal, 0)
    for d in range(diag_tiles):
        tiles(n_below + d, d * TK_SOFTMAX)
    lam = lam_ref[...]
    e1 = jnp.exp(jnp.sum(lam[0:1] * lam[1:2], axis=-1, keepdims=True))
    e2 = jnp.exp(jnp.sum(lam[2:3] * lam[3:4], axis=-1, keepdims=True))
    lam_full = e1 - e2 + lambda_init
    oT = acc0[...] * (1.0 / l0[...]) - lam_full * (acc1[...] * (1.0 / l1[...]))
    oT = oT * lax.rsqrt(jnp.mean(oT * oT, axis=0, keepdims=True) + RMS_EPS)
    oT = oT * sg_ref[...] * (1.0 - lambda_init)
    o_ref[0] = oT.T.astype(o_ref.dtype)


def _attn_fox_kernel(qT_ref, kT_ref, vT_ref, o_ref, m0, m1, l0, l1, acc0, acc1):
    qi = pl.program_id(2)
    ms, ls, accs = [m0, m1], [l0, l1], [acc0, acc1]
    _init_softmax(ms, ls, accs)

    def tiles(ki, diag):
        cols = _key_tile(ki, TK_SOFTMAX)
        for j in range(2):
            ks = slice(j * FOX_K, (j + 1) * FOX_K)
            vs = slice(j * HEAD_DIM, (j + 1) * HEAD_DIM)
            _softmax_tile(kT_ref[0, ks, cols], qT_ref, ks, vT_ref[0, vs, cols], ms[j], ls[j], accs[j], diag)

    def below_diagonal(ki, carry):
        tiles(ki, None)
        return carry

    diag_tiles = TQ // TK_SOFTMAX
    n_below = qi * diag_tiles
    lax.fori_loop(0, n_below, below_diagonal, 0)
    for d in range(diag_tiles):
        tiles(n_below + d, d * TK_SOFTMAX)
    oT = jnp.concatenate([acc0[...] * (1.0 / l0[...]), acc1[...] * (1.0 / l1[...])], axis=0)
    o_ref[0] = oT.T.astype(o_ref.dtype)


def _neg_abs(x):
    sign = jnp.uint32(0x80000000)
    return lax.bitcast_convert_type(lax.bitcast_convert_type(x, jnp.uint32) | sign, F32)


def _sb_tile(kT, qT_ref, q_rows, vT, later_ref, r_ref, acc_ref, diag):
    qs = _query_cols(diag)
    zT = _dot_tn(kT, qT_ref[0, q_rows, qs])
    sp = jnp.log2(1.0 + jnp.exp2(_neg_abs(zT)))
    log_beta = jnp.minimum(zT, 0.0) - sp
    neg_keep = jnp.maximum(zT, 0.0) + sp
    if diag is not None:
        valid = _diag_valid(zT.shape, True)
        neg_keep = jnp.where(valid, neg_keep, 0.0)
    laterT = _dot(later_ref[...], neg_keep.astype(BF16))
    r_prev = r_ref[:, qs]
    a = jnp.exp2(log_beta - laterT - r_prev)
    if diag is not None:
        a = jnp.where(valid, a, 0.0)
    acc_ref[:, qs] = acc_ref[:, qs] + _dot(vT, a.astype(BF16))
    r_ref[:, qs] = r_prev + laterT[0:1] + neg_keep[0:1]


def _attn_sb_kernel(qT_ref, kT_ref, vT_ref, later_ref, o_ref, r0, r1, acc0, acc1):
    qi = pl.program_id(2)
    rs, accs = [r0, r1], [acc0, acc1]
    for r in rs + accs:
        r[...] = jnp.zeros_like(r)

    def tiles(ki, diag):
        cols = _key_tile(ki, TK_STICK)
        for j in range(2):
            hs = slice(j * HEAD_DIM, (j + 1) * HEAD_DIM)
            _sb_tile(kT_ref[0, hs, cols], qT_ref, hs, vT_ref[0, hs, cols], later_ref, rs[j], accs[j], diag)

    diag_tiles = TQ // TK_STICK
    n_below = qi * diag_tiles

    def below_diagonal(i, carry):
        tiles(n_below - 1 - i, None)
        return carry

    for d in reversed(range(diag_tiles)):
        tiles(n_below + d, d * TK_STICK)
    lax.fori_loop(0, n_below, below_diagonal, 0)
    o_ref[0] = jnp.concatenate([acc0[...], acc1[...]], axis=0).T.astype(o_ref.dtype)


def _attention_call(body, name, n_groups, qT, kT, vT, qk_rows, v_rows, extra_in, extra_specs, scratch):
    bsz, _, seq = qT.shape
    in_specs = [
        pl.BlockSpec((1, qk_rows, TQ), lambda b, h, qi: (b, h, qi)),
        pl.BlockSpec((1, qk_rows, seq), lambda b, h, qi: (b, h, 0)),
        pl.BlockSpec((1, v_rows, seq), lambda b, h, qi: (b, h, 0)),
    ] + extra_specs
    return pl.pallas_call(
        body,
        grid=(bsz, n_groups, seq // TQ),
        in_specs=in_specs,
        out_specs=pl.BlockSpec((1, TQ, 2 * HEAD_DIM), lambda b, h, qi: (b, qi, h)),
        out_shape=jax.ShapeDtypeStruct((bsz, seq, n_groups * 2 * HEAD_DIM), BF16),
        scratch_shapes=scratch,
        name=name,
    )(qT, kT, vT, *extra_in)


def _stat_scratch(n_stats, v_dim):
    return [pltpu.VMEM((1, TQ), F32)] * n_stats + [pltpu.VMEM((v_dim, TQ), F32)] * 2


def _even_mixer_attention(qa, ka, va, qb, kb, vb, lam, subln_g, lambda_init):
    full = lambda shape: pl.BlockSpec(shape, lambda b, h, qi: (0,) * len(shape))
    pair = 2 * HEAD_DIM
    o_a = _attention_call(
        functools.partial(_attn_diff_kernel, lambda_init=lambda_init), "attn_diff", A_HEADS,
        qa, ka, va, pair, pair, [lam, subln_g.reshape(pair, 1)],
        [full((4, HEAD_DIM)), full((pair, 1))], _stat_scratch(4, pair))
    o_b = _attention_call(_attn_fox_kernel, "attn_fox", B_HEADS // 2, qb, kb, vb, 2 * FOX_K, pair,
                          [], [], _stat_scratch(4, HEAD_DIM))
    return [o_a, o_b]


def _sb_attention(qT, kT, vT):
    full = lambda shape: pl.BlockSpec(shape, lambda b, h, qi: (0,) * len(shape))
    pair = 2 * HEAD_DIM
    later = jnp.triu(jnp.ones((TK_STICK, TK_STICK), F32), 1).astype(BF16)
    return [_attention_call(_attn_sb_kernel, "attn_sb", C_HEADS // 2, qT, kT, vT, pair, pair,
                            [later], [full((TK_STICK, TK_STICK))], _stat_scratch(2, HEAD_DIM))]


def _out_proj_route_kernel(*refs, n_parts):
    o_refs = refs[:n_parts]
    wo_refs = refs[n_parts:2 * n_parts]
    x_ref, mod_ref, g_ref, wrh_ref, wrl_ref, br_ref, x1_o, h2_o, ids_o, wts_o = refs[2 * n_parts:]
    mod_rows = mod_ref[0]
    gate1 = mod_rows[2:3]
    mix = _dot(o_refs[0][0], wo_refs[0][...])
    for o_ref, wo_ref in zip(o_refs[1:], wo_refs[1:]):
        mix = mix + _dot(o_ref[0], wo_ref[...])
    x1 = x_ref[0] + gate1 * mix
    x1_o[0] = x1
    h2 = _adaln(x1, mod_rows, g_ref[...], 3)
    h2_o[0] = h2

    h_hi, h_lo = _split2(h2)
    wr_hi = wrh_ref[...]
    logits = _dot(h_hi, wr_hi) + _dot(h_hi, wrl_ref[...]) + _dot(h_lo, wr_hi) + br_ref[...]

    lane = lax.broadcasted_iota(jnp.int32, logits.shape, 1)
    big = jnp.int32(LANES)
    g_mask = lane < N_GROUPS
    g_log = jnp.where(g_mask, logits, NEG)
    g_max = jnp.max(g_log, axis=-1, keepdims=True)
    g_sum = jnp.sum(jnp.where(g_mask, jnp.exp(g_log - g_max), 0.0), axis=-1, keepdims=True)
    g_w = 1.0 / g_sum
    g_idx = jnp.min(jnp.where(g_mask & (g_log == g_max), lane, big), axis=-1, keepdims=True)

    e_mask = (lane >= ROUTE_E0) & (lane < ROUTE_E0 + N_EXP_TOTAL) & ((lane >> 3) == g_idx + 1)
    e_log = jnp.where(e_mask, logits, NEG)
    e_max = jnp.max(e_log, axis=-1, keepdims=True)
    e_exp = jnp.where(e_mask, jnp.exp(e_log - e_max), 0.0)
    e_prob = e_exp / jnp.sum(e_exp, axis=-1, keepdims=True)
    p1 = jnp.max(jnp.where(e_mask, e_prob, -1.0), axis=-1, keepdims=True)
    i1 = jnp.min(jnp.where(e_mask & (e_prob == p1), lane, big), axis=-1, keepdims=True)
    rest = e_mask & (lane != i1)
    p2 = jnp.max(jnp.where(rest, e_prob, -1.0), axis=-1, keepdims=True)
    i2 = jnp.min(jnp.where(rest & (e_prob == p2), lane, big), axis=-1, keepdims=True)
    den = p1 + p2
    w1 = p1 / den * g_w
    w2 = p2 / den * g_w
    ids_o[0] = jnp.where(lane == 0, i1 - ROUTE_E0, jnp.where(lane == 1, i2 - ROUTE_E0, 0))
    wts_o[0] = jnp.where(lane == 0, w1, jnp.where(lane == 1, w2, 0.0))


def _out_proj_route(o_parts, x, mod, g2, w_out, w_gr, b_gr, w_er, b_er):
    bsz, seq, d = x.shape
    widths = [o.shape[-1] for o in o_parts]
    starts = np.cumsum([0] + widths)
    wo_parts = [w_out[starts[i]:starts[i + 1]].astype(BF16) for i in range(len(o_parts))]
    wr = jnp.zeros((d, LANES), F32)
    wr = wr.at[:, :N_GROUPS].set(w_gr)
    wr = wr.at[:, ROUTE_E0:ROUTE_E0 + N_EXP_TOTAL].set(
        jnp.transpose(w_er, (1, 0, 2)).reshape(d, N_EXP_TOTAL))
    br = jnp.zeros((1, LANES), F32)
    br = br.at[0, :N_GROUPS].set(b_gr)
    br = br.at[0, ROUTE_E0:ROUTE_E0 + N_EXP_TOTAL].set(b_er.reshape(N_EXP_TOTAL))
    wr_hi = wr.astype(BF16)
    wr_lo = (wr - wr_hi.astype(F32)).astype(BF16)
    full = lambda shape: pl.BlockSpec(shape, lambda b, s: (0,) * len(shape))
    tile = lambda w: pl.BlockSpec((1, TM, w), lambda b, s: (b, s, 0))
    return pl.pallas_call(
        functools.partial(_out_proj_route_kernel, n_parts=len(o_parts)),
        grid=(bsz, seq // TM),
        in_specs=[tile(w) for w in widths] + [full((w, d)) for w in widths] + [
            tile(d),
            pl.BlockSpec((1, 6, d), lambda b, s: (b, 0, 0)),
            full((1, d)), full((d, LANES)), full((d, LANES)), full((1, LANES)),
        ],
        out_specs=[tile(d), tile(d), tile(LANES), tile(LANES)],
        out_shape=[
            jax.ShapeDtypeStruct((bsz, seq, d), F32),
            jax.ShapeDtypeStruct((bsz, seq, d), F32),
            jax.ShapeDtypeStruct((bsz, seq, LANES), jnp.int32),
            jax.ShapeDtypeStruct((bsz, seq, LANES), F32),
        ],
        name="out_proj_route",
    )(*o_parts, *wo_parts, x, mod, g2.reshape(1, d), wr_hi, wr_lo, br)


def _expert_onehots(ids_ref):
    tc = ids_ref.shape[2]
    expert = lax.broadcasted_iota(jnp.int32, (N_EXP_TOTAL, tc), 0)
    ids = ids_ref[0]
    return [jnp.where(expert == ids[k:k + 1], 1.0, 0.0) for k in range(2)]


def _rank_kernel(ids_ref, rank_o, cnt_o, carry):
    tc = ids_ref.shape[2]

    @pl.when(pl.program_id(0) == 0)
    def _():
        carry[...] = jnp.zeros_like(carry)

    r = lax.broadcasted_iota(jnp.int32, (tc, tc), 0)
    c = lax.broadcasted_iota(jnp.int32, (tc, tc), 1)
    before = jnp.where(r < c, 1.0, 0.0).astype(BF16)
    base = carry[...]
    rank_o[0] = jnp.zeros(rank_o.shape[1:], rank_o.dtype)
    for k, oh in enumerate(_expert_onehots(ids_ref)):
        prefix = _dot(oh.astype(BF16), before)
        rank_o[0, k:k + 1, :] = jnp.sum(oh * (base + prefix), axis=0, keepdims=True).astype(jnp.int32)
        base = base + jnp.sum(oh, axis=-1, keepdims=True)
    carry[...] = base
    cnt_o[...] = jnp.broadcast_to(base, cnt_o.shape)


def _pos_kernel(ids_ref, rank_ref, off_ref, pos_o):
    rank = rank_ref[0]
    pos_o[0] = jnp.zeros(pos_o.shape[1:], pos_o.dtype)
    for k, oh in enumerate(_expert_onehots(ids_ref)):
        off = jnp.sum(oh * off_ref[...], axis=0, keepdims=True).astype(jnp.int32)
        pos_o[0, k:k + 1, :] = off + rank[k:k + 1]


def _dispatch_plan(ids2):
    n = ids2.shape[0]
    nc = n // TC
    ids_t = jnp.zeros((nc, 8, TC), jnp.int32).at[:, :2].set(
        jnp.transpose(ids2.reshape(nc, TC, 2), (0, 2, 1)))
    blk = pl.BlockSpec((1, 8, TC), lambda c: (c, 0, 0))
    rank, cnt = pl.pallas_call(
        _rank_kernel,
        grid=(nc,),
        in_specs=[blk],
        out_specs=[blk, pl.BlockSpec((N_EXP_TOTAL, LANES), lambda c: (0, 0))],
        out_shape=[jax.ShapeDtypeStruct((nc, 8, TC), jnp.int32),
                   jax.ShapeDtypeStruct((N_EXP_TOTAL, LANES), F32)],
        scratch_shapes=[pltpu.VMEM((N_EXP_TOTAL, 1), F32)],
        name="moe_rank",
    )(ids_t)
    counts = cnt[:, 0].astype(jnp.int32)
    tiles_per = (counts + T_EXP - 1) // T_EXP
    tile_start = jnp.cumsum(tiles_per) - tiles_per
    n_used = jnp.sum(tiles_per)
    pos = pl.pallas_call(
        _pos_kernel,
        grid=(nc,),
        in_specs=[blk, blk, pl.BlockSpec((N_EXP_TOTAL, 1), lambda c: (0, 0))],
        out_specs=blk,
        out_shape=jax.ShapeDtypeStruct((nc, 8, TC), jnp.int32),
        name="moe_pos",
    )(ids_t, rank, (tile_start * T_EXP).astype(F32).reshape(N_EXP_TOTAL, 1))
    n_tiles = (2 * n) // T_EXP + N_EXP_TOTAL
    tile_idx = jnp.minimum(jnp.arange(n_tiles, dtype=jnp.int32), n_used - 1)
    tile_expert = jnp.sum((tile_start[None, :] <= tile_idx[:, None]).astype(jnp.int32), axis=1) - 1
    pos_chunks = pos[:, :2].reshape(nc, 1, 2 * TC)
    return pos_chunks, tile_expert, n_used.reshape(1).astype(jnp.int32), n_tiles


def _row_copy_wait(src, dst, sem, rows):
    pltpu.make_async_copy(src.at[pl.ds(0, rows)], dst.at[pl.ds(0, rows)], sem).wait()


def _dispatch_kernel(pos_ref, h_ref, xs_in_hbm, xs_hbm, sem):
    del xs_in_hbm

    def issue(i, carry):
        for k in range(2):
            dst = pos_ref[0, 0, k * TC + i]
            pltpu.make_async_copy(h_ref.at[pl.ds(i, 1)], xs_hbm.at[pl.ds(dst, 1)], sem).start()
        return carry

    lax.fori_loop(0, TC, issue, 0, unroll=ISSUE_UNROLL)
    for k in range(2):
        _row_copy_wait(h_ref, xs_hbm, sem, TC)


def _dispatch(h2_flat, pos_chunks, n_rows):
    n, d = h2_flat.shape
    xs0 = jnp.zeros((n_rows, d), h2_flat.dtype)
    return pl.pallas_call(
        _dispatch_kernel,
        grid=(n // TC,),
        in_specs=[
            pl.BlockSpec((1, 1, 2 * TC), lambda c: (c, 0, 0), memory_space=pltpu.SMEM),
            pl.BlockSpec((TC, d), lambda c: (c, 0)),
            pl.BlockSpec(memory_space=pl.ANY),
        ],
        out_specs=pl.BlockSpec(memory_space=pl.ANY),
        out_shape=jax.ShapeDtypeStruct((n_rows, d), h2_flat.dtype),
        scratch_shapes=[pltpu.SemaphoreType.DMA(())],
        input_output_aliases={2: 0},
        name="moe_dispatch",
    )(pos_chunks, h2_flat, xs0)


def _expert_kernel(te_ref, nu_ref, xs_ref, w1_ref, w3_ref, w2_ref, ys_ref, w1_s, w3_s, w2_s):
    j = pl.program_id(0)
    prev = te_ref[jnp.maximum(j - 1, 0)]

    @pl.when((j == 0) | (te_ref[j] != prev))
    def _():
        w1_s[...] = w1_ref[0].astype(BF16)
        w3_s[...] = w3_ref[0].astype(BF16)
        w2_s[...] = w2_ref[0].astype(BF16)

    @pl.when(j < nu_ref[0])
    def _():
        xb = xs_ref[...].astype(BF16)
        a = _dot(xb, w1_s[...])
        b = _dot(xb, w3_s[...])
        hid = a * jax.nn.sigmoid(a) * b
        ys_ref[...] = _dot(hid.astype(BF16), w2_s[...])

    @pl.when(j >= nu_ref[0])
    def _():
        ys_ref[...] = jnp.zeros_like(ys_ref)


def _experts(xs, tile_expert, n_used, w1, w3, w2, layer):
    n_rows, d = xs.shape
    f = w1.shape[-1]
    n_all = w1.shape[0] * N_EXP_TOTAL
    tile_expert = tile_expert + layer * N_EXP_TOTAL
    grid_spec = pltpu.PrefetchScalarGridSpec(
        num_scalar_prefetch=2,
        grid=(n_rows // T_EXP,),
        in_specs=[
            pl.BlockSpec((T_EXP, d), lambda j, te, nu: (j, 0)),
            pl.BlockSpec((1, d, f), lambda j, te, nu: (te[j], 0, 0)),
            pl.BlockSpec((1, d, f), lambda j, te, nu: (te[j], 0, 0)),
            pl.BlockSpec((1, f, d), lambda j, te, nu: (te[j], 0, 0)),
        ],
        out_specs=pl.BlockSpec((T_EXP, d), lambda j, te, nu: (j, 0)),
        scratch_shapes=[pltpu.VMEM((d, f), BF16), pltpu.VMEM((d, f), BF16), pltpu.VMEM((f, d), BF16)],
    )
    return pl.pallas_call(
        _expert_kernel,
        grid_spec=grid_spec,
        out_shape=jax.ShapeDtypeStruct((n_rows, d), F32),
        name="moe_experts",
    )(tile_expert, n_used, xs, w1.reshape(n_all, d, f), w3.reshape(n_all, d, f), w2.reshape(n_all, f, d))


def _combine_kernel(pos_ref, ys_hbm, x1_ref, wts_ref, mod_ref, x2_o, buf, sem):
    def issue(i, carry):
        for k in range(2):
            src = pos_ref[0, 0, k * TC + i]
            pltpu.make_async_copy(ys_hbm.at[pl.ds(src, 1)], buf.at[k, pl.ds(i, 1)], sem).start()
        return carry

    lax.fori_loop(0, TC, issue, 0, unroll=ISSUE_UNROLL)
    for k in range(2):
        _row_copy_wait(ys_hbm, buf.at[k], sem, TC)
    wts = wts_ref[0]
    y = wts[:, 0:1] * buf[0] + wts[:, 1:2] * buf[1]
    gate2 = mod_ref[0][5:6]
    x2_o[0] = x1_ref[0] + gate2 * y


def _combine(ys, pos_chunks, x1, wts, mod):
    bsz, seq, d = x1.shape
    per_b = seq // TC
    return pl.pallas_call(
        _combine_kernel,
        grid=(bsz, per_b),
        in_specs=[
            pl.BlockSpec((1, 1, 2 * TC), lambda b, s: (b * per_b + s, 0, 0), memory_space=pltpu.SMEM),
            pl.BlockSpec(memory_space=pl.ANY),
            pl.BlockSpec((1, TC, d), lambda b, s: (b, s, 0)),
            pl.BlockSpec((1, TC, LANES), lambda b, s: (b, s, 0)),
            pl.BlockSpec((1, 6, d), lambda b, s: (b, 0, 0)),
        ],
        out_specs=pl.BlockSpec((1, TC, d), lambda b, s: (b, s, 0)),
        out_shape=jax.ShapeDtypeStruct((bsz, seq, d), F32),
        scratch_shapes=[pltpu.VMEM((2, TC, d), F32), pltpu.SemaphoreType.DMA(())],
        name="moe_combine",
    )(pos_chunks, ys, x1, wts, mod)


def _moe(h2, ids, wts, x1, mod, w1, w3, w2, layer):
    bsz, seq, d = h2.shape
    n = bsz * seq
    pos_chunks, tile_expert, n_used, n_tiles = _dispatch_plan(ids.reshape(n, LANES)[:, :2])
    xs = _dispatch(h2.reshape(n, d), pos_chunks, n_tiles * T_EXP)
    ys = _experts(xs, tile_expert, n_used, w1, w3, w2, layer)
    return _combine(ys, pos_chunks, x1, wts, mod)


def kernel(x, c, positions, mod_w, mod_b, norm1_g, norm2_g, ev_w_in, ev_b_f, ev_qn_a, ev_kn_a, ev_lam,
           ev_subln_g, ev_qn_b, ev_kn_b, ev_w_out, od_w_in, od_w_out, moe_w_gr, moe_b_gr, moe_w_er,
           moe_b_er, moe_w1, moe_w3, moe_w2):
    depth = mod_w.shape[0]
    mod = _modulation(c, mod_w, mod_b)
    cos_t, sin_t = _rope_tables(positions)
    for l in range(depth):
        if l % 2 == 0:
            e = l // 2
            lambda_init = 0.8 - 0.6 * math.exp(-0.3 * l)
            qa, ka, va, qb, kb, vb = _ln_proj_even(
                x, mod[l], norm1_g[l], ev_w_in[e], ev_b_f[e], ev_qn_a[e], ev_kn_a[e], ev_qn_b[e],
                ev_kn_b[e], cos_t, sin_t)
            o = _even_mixer_attention(qa, ka, va, qb, kb, vb, ev_lam[e], ev_subln_g[e], lambda_init)
            w_out = ev_w_out[e]
        else:
            od = l // 2
            q, k, v = _ln_proj_odd(x, mod[l], norm1_g[l], od_w_in[od])
            o = _sb_attention(q, k, v)
            w_out = od_w_out[od]
        x1, h2, ids, wts = _out_proj_route(o, x, mod[l], norm2_g[l], w_out, moe_w_gr[l], moe_b_gr[l],
                                           moe_w_er[l], moe_b_er[l])
        x = _moe(h2, ids, wts, x1, mod[l], moe_w1, moe_w3, moe_w2, l)
    return x
```

```python
import functools
import math

import numpy as np
import jax
import jax.numpy as jnp
from jax import lax
from jax.experimental import pallas as pl
from jax.experimental.pallas import tpu as pltpu

F32 = jnp.float32
BF16 = jnp.bfloat16

D_MODEL = 1024
HEAD_DIM = 64
HALF = HEAD_DIM // 2
A_HEADS = 4
B_HEADS = 8
C_HEADS = 16
SEC = 512
N_GROUPS = 4
N_EXPERTS = 8
N_EXP_TOTAL = N_GROUPS * N_EXPERTS
ROPE_THETA = 10000.0
RMS_EPS = 1e-6
QK_SCALE = HEAD_DIM ** -0.5
LOG2E = math.log2(math.e)

LANES = 128
TM = 512
TQ_SOFTMAX = 1024
TQ_STICK = 1024
TK_SOFTMAX = 1024
TK_STICK = 512
FOX_AUG = 16
FOX_K = HEAD_DIM + FOX_AUG
T_EXP = 256
TC = 256
RANK_TC = 1024
ISSUE_UNROLL = 8
NEG = -1e30
ROUTE_E0 = 8

NT_DIMS = (((1,), (1,)), ((), ()))
TN_DIMS = (((0,), (0,)), ((), ()))


def _dot(a, b):
    return jnp.dot(a, b, preferred_element_type=F32)


def _dot_nt(a, b):
    return lax.dot_general(a, b, NT_DIMS, preferred_element_type=F32)


def _dot_tn(a, b):
    return lax.dot_general(a, b, TN_DIMS, preferred_element_type=F32)


def _split2(x):
    hi = x.astype(BF16)
    lo = (x - hi.astype(F32)).astype(BF16)
    return hi, lo


def _split3(x):
    hi = x.astype(BF16)
    r = x - hi.astype(F32)
    mid = r.astype(BF16)
    lo = (r - mid.astype(F32)).astype(BF16)
    return hi, mid, lo


def _softplus_neg_abs(z):
    return jnp.log(1.0 + jnp.exp(-jnp.abs(z)))


def _log_sigmoid(z):
    return jnp.minimum(z, 0.0) - _softplus_neg_abs(z)


def _rms_rows(x, eps=RMS_EPS):
    return x * lax.rsqrt(jnp.mean(x * x, axis=-1, keepdims=True) + eps)


def _mod_kernel(c_ref, w_ref, b_ref, o_ref):
    c = c_ref[...]
    ca = c * jax.nn.sigmoid(c)
    c_hi, c_mid, c_lo = _split3(ca)
    w = w_ref[0]
    w_hi, w_lo = _split2(w)
    acc = _dot(c_hi, w_hi) + _dot(c_hi, w_lo) + _dot(c_mid, w_hi) + _dot(c_lo, w_hi) + _dot(c_mid, w_lo)
    o_ref[0] = acc + b_ref[0]


def _modulation(c, mod_w, mod_b):
    depth, d, n6 = mod_w.shape
    bsz = c.shape[0]
    rows = 8
    tn = 1536
    c_pad = jnp.zeros((rows, d), F32).at[:bsz].set(c)
    out = pl.pallas_call(
        _mod_kernel,
        grid=(depth, n6 // tn),
        in_specs=[
            pl.BlockSpec((rows, d), lambda l, j: (0, 0)),
            pl.BlockSpec((1, d, tn), lambda l, j: (l, 0, j)),
            pl.BlockSpec((1, 1, tn), lambda l, j: (l, 0, j)),
        ],
        out_specs=pl.BlockSpec((1, rows, tn), lambda l, j: (l, 0, j)),
        out_shape=jax.ShapeDtypeStruct((depth, rows, n6), F32),
        name="adaln_mod",
    )(c_pad, mod_w, mod_b.reshape(depth, 1, n6))
    return out[:, :bsz].reshape(depth, bsz, 6, d)


def _rope_table_kernel(pos_ref, inv_ref, cos_ref, sin_ref):
    ang = pos_ref[0].astype(F32) * inv_ref[...]
    cos_ref[0] = jnp.cos(ang)
    sin_ref[0] = jnp.sin(ang)


def _rope_tables(positions):
    bsz, seq = positions.shape
    ts = 2048
    inv = ROPE_THETA ** (-2.0 * jnp.arange(HALF, dtype=F32) / HEAD_DIM)
    return pl.pallas_call(
        _rope_table_kernel,
        grid=(bsz, seq // ts),
        in_specs=[
            pl.BlockSpec((1, 1, ts), lambda b, s: (b, 0, s)),
            pl.BlockSpec((HALF, 1), lambda b, s: (0, 0)),
        ],
        out_specs=[pl.BlockSpec((1, HALF, ts), lambda b, s: (b, 0, s))] * 2,
        out_shape=[jax.ShapeDtypeStruct((bsz, HALF, seq), F32)] * 2,
        name="rope_tables",
    )(positions.reshape(bsz, 1, seq), inv.reshape(HALF, 1))


def _adaln(x, mod_rows, g, first):
    shift = mod_rows[first:first + 1]
    scale = mod_rows[first + 1:first + 2]
    return _rms_rows(x) * g * (1.0 + scale) + shift


def _ln_proj_even_kernel(x_ref, mod_ref, g_ref, wT_ref, wfT_ref, bf_ref,
                         qna_ref, kna_ref, qnb_ref, knb_ref, cos_ref, sin_ref,
                         qa_o, ka_o, va_o, qb_o, kb_o, vb_o, carry):
    tm = x_ref.shape[1]

    @pl.when(pl.program_id(1) == 0)
    def _():
        carry[...] = jnp.zeros_like(carry)

    hb = _adaln(x_ref[0], mod_ref[0], g_ref[...], 0).astype(BF16)
    cos = cos_ref[0]
    sin = sin_ref[0]

    def section(idx):
        return _dot_nt(wT_ref[idx * SEC:(idx + 1) * SEC, :], hb)

    def norm_heads(p, g_col, out_ref, rope, scale, stride):
        for j in range(SEC // HEAD_DIM):
            xj = p[j * HEAD_DIM:(j + 1) * HEAD_DIM]
            yj = xj * lax.rsqrt(jnp.mean(xj * xj, axis=0, keepdims=True) + RMS_EPS) * g_col
            if rope:
                y1 = yj[:HALF]
                y2 = yj[HALF:]
                yj = jnp.concatenate([y1 * cos - y2 * sin, y2 * cos + y1 * sin], axis=0)
            out_ref[0, j * stride:j * stride + HEAD_DIM, :] = (yj * scale).astype(out_ref.dtype)

    softmax_q_scale = QK_SCALE * LOG2E
    norm_heads(section(0), qna_ref[...], qa_o, True, softmax_q_scale, HEAD_DIM)
    norm_heads(section(1), kna_ref[...], ka_o, True, 1.0, HEAD_DIM)
    va_o[0] = section(2).astype(va_o.dtype)
    norm_heads(section(3), qnb_ref[...], qb_o, False, softmax_q_scale, FOX_K)
    norm_heads(section(4), knb_ref[...], kb_o, False, 1.0, FOX_K)
    vb_o[0] = section(5).astype(vb_o.dtype)

    r = lax.broadcasted_iota(jnp.int32, (tm, tm), 0)
    c = lax.broadcasted_iota(jnp.int32, (tm, tm), 1)
    upto = jnp.where(r <= c, 1.0, 0.0).astype(BF16)
    log_f = _log_sigmoid(_dot_nt(wfT_ref[...], hb)[:B_HEADS] + bf_ref[...])
    f_hi, f_mid, f_lo = _split3(log_f)
    cum = _dot(f_hi, upto) + _dot(f_mid, upto) + _dot(f_lo, upto) + carry[...]
    carry[...] = cum[:, tm - 1:tm]

    row = lax.broadcasted_iota(jnp.int32, (FOX_AUG, tm), 0)
    for j in range(B_HEADS):
        pieces = [p.astype(F32) for p in _split3(cum[j:j + 1] * LOG2E)]
        q_aug = jnp.where(row < 3, 1.0, 0.0)
        k_aug = jnp.where((row >= 3) & (row < 6), 1.0, 0.0)
        for i, piece in enumerate(pieces):
            q_aug = jnp.where(row == 3 + i, piece, q_aug)
            k_aug = jnp.where(row == i, -piece, k_aug)
        lo = j * FOX_K + HEAD_DIM
        qb_o[0, lo:lo + FOX_AUG, :] = q_aug.astype(qb_o.dtype)
        kb_o[0, lo:lo + FOX_AUG, :] = k_aug.astype(kb_o.dtype)


def _ln_proj_even(x, mod, g, w_in, b_f, qn_a, kn_a, qn_b, kn_b, cos_t, sin_t):
    bsz, seq, d = x.shape
    n_main = 6 * SEC
    wT = w_in[:, :n_main].T.astype(BF16)
    wfT = jnp.zeros((16, d), F32).at[:B_HEADS].set(w_in[:, n_main:].T).astype(BF16)
    col = lambda v: v.reshape(HEAD_DIM, 1)
    full = lambda shape: pl.BlockSpec(shape, lambda b, s: (0,) * len(shape))
    rows_spec = lambda rows: pl.BlockSpec((1, rows, TM), lambda b, s: (b, 0, s))
    rows_shape = lambda rows: jax.ShapeDtypeStruct((bsz, rows, seq), BF16)
    out_rows = [SEC, SEC, SEC, B_HEADS * FOX_K, B_HEADS * FOX_K, SEC]
    return pl.pallas_call(
        _ln_proj_even_kernel,
        grid=(bsz, seq // TM),
        in_specs=[
            pl.BlockSpec((1, TM, d), lambda b, s: (b, s, 0)),
            pl.BlockSpec((1, 6, d), lambda b, s: (b, 0, 0)),
            full((1, d)), full((n_main, d)), full((16, d)), full((B_HEADS, 1)),
            full((HEAD_DIM, 1)), full((HEAD_DIM, 1)), full((HEAD_DIM, 1)), full((HEAD_DIM, 1)),
            rows_spec(HALF), rows_spec(HALF),
        ],
        out_specs=[rows_spec(n) for n in out_rows],
        out_shape=[rows_shape(n) for n in out_rows],
        scratch_shapes=[pltpu.VMEM((B_HEADS, 1), F32)],
        name="ln_proj_even",
    )(x, mod, g.reshape(1, d), wT, wfT, b_f.reshape(B_HEADS, 1), col(qn_a), col(kn_a), col(qn_b),
      col(kn_b), cos_t, sin_t)


def _ln_proj_odd_kernel(x_ref, mod_ref, g_ref, wT_ref, q_o, k_o, v_o):
    hb = _adaln(x_ref[0], mod_ref[0], g_ref[...], 0).astype(BF16)
    width = q_o.shape[1]
    for idx, (out_ref, scale) in enumerate(((q_o, QK_SCALE * LOG2E), (k_o, 1.0), (v_o, 1.0))):
        for half in range(2):
            lo = idx * width + half * (width // 2)
            p = _dot_nt(wT_ref[lo:lo + width // 2, :], hb)
            out_ref[0, half * (width // 2):(half + 1) * (width // 2), :] = (p * scale).astype(out_ref.dtype)


def _ln_proj_odd(x, mod, g, w_in):
    bsz, seq, d = x.shape
    width = C_HEADS * HEAD_DIM
    wT = w_in.T.astype(BF16)
    full = lambda shape: pl.BlockSpec(shape, lambda b, s: (0,) * len(shape))
    spec = pl.BlockSpec((1, width, TM), lambda b, s: (b, 0, s))
    shape = jax.ShapeDtypeStruct((bsz, width, seq), BF16)
    return pl.pallas_call(
        _ln_proj_odd_kernel,
        grid=(bsz, seq // TM),
        in_specs=[
            pl.BlockSpec((1, TM, d), lambda b, s: (b, s, 0)),
            pl.BlockSpec((1, 6, d), lambda b, s: (b, 0, 0)),
            full((1, d)), full((3 * width, d)),
        ],
        out_specs=[spec] * 3,
        out_shape=[shape] * 3,
        name="ln_proj_odd",
    )(x, mod, g.reshape(1, d), wT)


def _diag_valid(shape, strict):
    key = lax.broadcasted_iota(jnp.int32, shape, 0)
    qry = lax.broadcasted_iota(jnp.int32, shape, 1)
    return (qry > key) if strict else (qry >= key)


def _query_cols(diag):
    return slice(0 if diag is None else diag, None)


def _softmax_tile(kT, qT_ref, q_rows, vT, m_ref, l_ref, acc_ref, diag):
    qs = _query_cols(diag)
    sT = _dot_tn(kT, qT_ref[0, q_rows, qs])
    if diag is not None:
        sT = jnp.where(_diag_valid(sT.shape, False), sT, NEG)
    m_prev = m_ref[:, qs]
    m_new = jnp.maximum(m_prev, jnp.max(sT, axis=0, keepdims=True))
    alpha = jnp.exp2(m_prev - m_new)
    p = jnp.exp2(sT - m_new)
    l_ref[:, qs] = alpha * l_ref[:, qs] + jnp.sum(p, axis=0, keepdims=True)
    acc_ref[:, qs] = alpha * acc_ref[:, qs] + _dot(vT, p.astype(BF16))
    m_ref[:, qs] = m_new


def _init_softmax(m_refs, l_refs, acc_refs):
    for r in m_refs:
        r[...] = jnp.full_like(r, -jnp.inf)
    for r in l_refs + acc_refs:
        r[...] = jnp.zeros_like(r)


def _key_tile(ki, tk):
    return pl.ds(pl.multiple_of(ki * tk, tk), tk)


def _attn_diff_kernel(qT_ref, kT_ref, vT_ref, lam_ref, sg_ref, o_ref,
                      m0, m1, l0, l1, acc0, acc1, *, lambda_init):
    qi = pl.program_id(2)
    ms, ls, accs = [m0, m1], [l0, l1], [acc0, acc1]
    _init_softmax(ms, ls, accs)

    def tiles(ki, diag):
        cols = _key_tile(ki, TK_SOFTMAX)
        for c in range(2):
            hs = slice(c * HEAD_DIM, (c + 1) * HEAD_DIM)
            _softmax_tile(kT_ref[0, hs, cols], qT_ref, hs, vT_ref[0, :, cols], ms[c], ls[c], accs[c], diag)

    def below_diagonal(ki, carry):
        tiles(ki, None)
        return carry

    diag_tiles = qT_ref.shape[2] // TK_SOFTMAX
    n_below = qi * diag_tiles
    lax.fori_loop(0, n_below, below_diagonal, 0)
    for d in range(diag_tiles):
        tiles(n_below + d, d * TK_SOFTMAX)
    lam = lam_ref[...]
    e1 = jnp.exp(jnp.sum(lam[0:1] * lam[1:2], axis=-1, keepdims=True))
    e2 = jnp.exp(jnp.sum(lam[2:3] * lam[3:4], axis=-1, keepdims=True))
    lam_full = e1 - e2 + lambda_init
    oT = acc0[...] * (1.0 / l0[...]) - lam_full * (acc1[...] * (1.0 / l1[...]))
    oT = oT * lax.rsqrt(jnp.mean(oT * oT, axis=0, keepdims=True) + RMS_EPS)
    oT = oT * sg_ref[...] * (1.0 - lambda_init)
    o_ref[0] = oT.T.astype(o_ref.dtype)


def _attn_fox_kernel(qT_ref, kT_ref, vT_ref, o_ref, m0, m1, l0, l1, acc0, acc1):
    qi = pl.program_id(2)
    ms, ls, accs = [m0, m1], [l0, l1], [acc0, acc1]
    _init_softmax(ms, ls, accs)

    def tiles(ki, diag):
        cols = _key_tile(ki, TK_SOFTMAX)
        for j in range(2):
            ks = slice(j * FOX_K, (j + 1) * FOX_K)
            vs = slice(j * HEAD_DIM, (j + 1) * HEAD_DIM)
            _softmax_tile(kT_ref[0, ks, cols], qT_ref, ks, vT_ref[0, vs, cols], ms[j], ls[j], accs[j], diag)

    def below_diagonal(ki, carry):
        tiles(ki, None)
        return carry

    diag_tiles = qT_ref.shape[2] // TK_SOFTMAX
    n_below = qi * diag_tiles
    lax.fori_loop(0, n_below, below_diagonal, 0)
    for d in range(diag_tiles):
        tiles(n_below + d, d * TK_SOFTMAX)
    oT = jnp.concatenate([acc0[...] * (1.0 / l0[...]), acc1[...] * (1.0 / l1[...])], axis=0)
    o_ref[0] = oT.T.astype(o_ref.dtype)


def _sb_tile(kT, qT_ref, q_rows, vT, later_ref, r_ref, acc_ref, diag):
    qs = _query_cols(diag)
    zT = _dot_tn(kT, qT_ref[0, q_rows, qs])
    neg_keep = jnp.maximum(zT, 0.0) + jnp.log2(1.0 + jnp.exp2(-jnp.abs(zT)))
    log_beta = zT - neg_keep
    if diag is not None:
        valid = _diag_valid(zT.shape, True)
        neg_keep = jnp.where(valid, neg_keep, 0.0)
    laterT = _dot(later_ref[...], neg_keep.astype(BF16))
    r_prev = r_ref[:, qs]
    a = jnp.exp2(log_beta - laterT - r_prev)
    if diag is not None:
        a = jnp.where(valid, a, 0.0)
    acc_ref[:, qs] = acc_ref[:, qs] + _dot(vT, a.astype(BF16))
    r_ref[:, qs] = r_prev + laterT[0:1] + neg_keep[0:1]


def _attn_sb_kernel(qT_ref, kT_ref, vT_ref, later_ref, o_ref, r0, r1, acc0, acc1):
    qi = pl.program_id(2)
    rs, accs = [r0, r1], [acc0, acc1]
    for r in rs + accs:
        r[...] = jnp.zeros_like(r)

    def tiles(ki, diag):
        cols = _key_tile(ki, TK_STICK)
        for j in range(2):
            hs = slice(j * HEAD_DIM, (j + 1) * HEAD_DIM)
            _sb_tile(kT_ref[0, hs, cols], qT_ref, hs, vT_ref[0, hs, cols], later_ref, rs[j], accs[j], diag)

    diag_tiles = qT_ref.shape[2] // TK_STICK
    n_below = qi * diag_tiles

    def below_diagonal(i, carry):
        tiles(n_below - 1 - i, None)
        return carry

    for d in reversed(range(diag_tiles)):
        tiles(n_below + d, d * TK_STICK)
    lax.fori_loop(0, n_below, below_diagonal, 0)
    o_ref[0] = jnp.concatenate([acc0[...], acc1[...]], axis=0).T.astype(o_ref.dtype)


def _attention_call(body, name, n_groups, tq, qT, kT, vT, qk_rows, v_rows, extra_in, extra_specs, n_stats,
                    v_dim):
    bsz, _, seq = qT.shape
    in_specs = [
        pl.BlockSpec((1, qk_rows, tq), lambda b, h, qi: (b, h, qi)),
        pl.BlockSpec((1, qk_rows, seq), lambda b, h, qi: (b, h, 0)),
        pl.BlockSpec((1, v_rows, seq), lambda b, h, qi: (b, h, 0)),
    ] + extra_specs
    scratch = [pltpu.VMEM((1, tq), F32)] * n_stats + [pltpu.VMEM((v_dim, tq), F32)] * 2
    return pl.pallas_call(
        body,
        grid=(bsz, n_groups, seq // tq),
        in_specs=in_specs,
        out_specs=pl.BlockSpec((1, tq, 2 * HEAD_DIM), lambda b, h, qi: (b, qi, h)),
        out_shape=jax.ShapeDtypeStruct((bsz, seq, n_groups * 2 * HEAD_DIM), BF16),
        scratch_shapes=scratch,
        name=name,
    )(qT, kT, vT, *extra_in)


def _even_mixer_attention(qa, ka, va, qb, kb, vb, lam, subln_g, lambda_init):
    full = lambda shape: pl.BlockSpec(shape, lambda b, h, qi: (0,) * len(shape))
    pair = 2 * HEAD_DIM
    o_a = _attention_call(
        functools.partial(_attn_diff_kernel, lambda_init=lambda_init), "attn_diff", A_HEADS, TQ_SOFTMAX,
        qa, ka, va, pair, pair, [lam, subln_g.reshape(pair, 1)],
        [full((4, HEAD_DIM)), full((pair, 1))], 4, pair)
    o_b = _attention_call(_attn_fox_kernel, "attn_fox", B_HEADS // 2, TQ_SOFTMAX, qb, kb, vb, 2 * FOX_K, pair,
                          [], [], 4, HEAD_DIM)
    return [o_a, o_b]


def _sb_attention(qT, kT, vT):
    full = lambda shape: pl.BlockSpec(shape, lambda b, h, qi: (0,) * len(shape))
    pair = 2 * HEAD_DIM
    later = jnp.triu(jnp.ones((TK_STICK, TK_STICK), F32), 1).astype(BF16)
    return [_attention_call(_attn_sb_kernel, "attn_sb", C_HEADS // 2, TQ_STICK, qT, kT, vT, pair, pair,
                            [later], [full((TK_STICK, TK_STICK))], 2, HEAD_DIM)]


def _out_proj_route_kernel(*refs, n_parts):
    o_refs = refs[:n_parts]
    wo_refs = refs[n_parts:2 * n_parts]
    x_ref, mod_ref, g_ref, wrh_ref, wrl_ref, br_ref, x1_o, h2_o, ids_o, wts_o = refs[2 * n_parts:]
    mod_rows = mod_ref[0]
    gate1 = mod_rows[2:3]
    mix = _dot(o_refs[0][0], wo_refs[0][...])
    for o_ref, wo_ref in zip(o_refs[1:], wo_refs[1:]):
        mix = mix + _dot(o_ref[0], wo_ref[...])
    x1 = x_ref[0] + gate1 * mix
    x1_o[0] = x1
    h2 = _adaln(x1, mod_rows, g_ref[...], 3)
    h2_o[0] = h2

    h_hi, h_lo = _split2(h2)
    wr_hi = wrh_ref[...]
    logits = _dot(h_hi, wr_hi) + _dot(h_hi, wrl_ref[...]) + _dot(h_lo, wr_hi) + br_ref[...]

    lane = lax.broadcasted_iota(jnp.int32, logits.shape, 1)
    big = jnp.int32(LANES)
    g_mask = lane < N_GROUPS
    g_log = jnp.where(g_mask, logits, NEG)
    g_max = jnp.max(g_log, axis=-1, keepdims=True)
    g_sum = jnp.sum(jnp.where(g_mask, jnp.exp(g_log - g_max), 0.0), axis=-1, keepdims=True)
    g_w = 1.0 / g_sum
    g_idx = jnp.min(jnp.where(g_mask & (g_log == g_max), lane, big), axis=-1, keepdims=True)

    e_mask = (lane >= ROUTE_E0) & (lane < ROUTE_E0 + N_EXP_TOTAL) & ((lane >> 3) == g_idx + 1)
    e_log = jnp.where(e_mask, logits, NEG)
    e_max = jnp.max(e_log, axis=-1, keepdims=True)
    e_exp = jnp.where(e_mask, jnp.exp(e_log - e_max), 0.0)
    e_prob = e_exp / jnp.sum(e_exp, axis=-1, keepdims=True)
    p1 = jnp.max(jnp.where(e_mask, e_prob, -1.0), axis=-1, keepdims=True)
    i1 = jnp.min(jnp.where(e_mask & (e_prob == p1), lane, big), axis=-1, keepdims=True)
    rest = e_mask & (lane != i1)
    p2 = jnp.max(jnp.where(rest, e_prob, -1.0), axis=-1, keepdims=True)
    i2 = jnp.min(jnp.where(rest & (e_prob == p2), lane, big), axis=-1, keepdims=True)
    den = p1 + p2
    w1 = p1 / den * g_w
    w2 = p2 / den * g_w
    ids_o[0] = jnp.where(lane == 0, i1 - ROUTE_E0, jnp.where(lane == 1, i2 - ROUTE_E0, 0))
    wts_o[0] = jnp.where(lane == 0, w1, jnp.where(lane == 1, w2, 0.0))


def _out_proj_route(o_parts, x, mod, g2, w_out, w_gr, b_gr, w_er, b_er):
    bsz, seq, d = x.shape
    widths = [o.shape[-1] for o in o_parts]
    starts = np.cumsum([0] + widths)
    wo_parts = [w_out[starts[i]:starts[i + 1]].astype(BF16) for i in range(len(o_parts))]
    wr = jnp.zeros((d, LANES), F32)
    wr = wr.at[:, :N_GROUPS].set(w_gr)
    wr = wr.at[:, ROUTE_E0:ROUTE_E0 + N_EXP_TOTAL].set(
        jnp.transpose(w_er, (1, 0, 2)).reshape(d, N_EXP_TOTAL))
    br = jnp.zeros((1, LANES), F32)
    br = br.at[0, :N_GROUPS].set(b_gr)
    br = br.at[0, ROUTE_E0:ROUTE_E0 + N_EXP_TOTAL].set(b_er.reshape(N_EXP_TOTAL))
    wr_hi = wr.astype(BF16)
    wr_lo = (wr - wr_hi.astype(F32)).astype(BF16)
    full = lambda shape: pl.BlockSpec(shape, lambda b, s: (0,) * len(shape))
    tile = lambda w: pl.BlockSpec((1, TM, w), lambda b, s: (b, s, 0))
    return pl.pallas_call(
        functools.partial(_out_proj_route_kernel, n_parts=len(o_parts)),
        grid=(bsz, seq // TM),
        in_specs=[tile(w) for w in widths] + [full((w, d)) for w in widths] + [
            tile(d),
            pl.BlockSpec((1, 6, d), lambda b, s: (b, 0, 0)),
            full((1, d)), full((d, LANES)), full((d, LANES)), full((1, LANES)),
        ],
        out_specs=[tile(d), tile(d), tile(LANES), tile(LANES)],
        out_shape=[
            jax.ShapeDtypeStruct((bsz, seq, d), F32),
            jax.ShapeDtypeStruct((bsz, seq, d), F32),
            jax.ShapeDtypeStruct((bsz, seq, LANES), jnp.int32),
            jax.ShapeDtypeStruct((bsz, seq, LANES), F32),
        ],
        name="out_proj_route",
    )(*o_parts, *wo_parts, x, mod, g2.reshape(1, d), wr_hi, wr_lo, br)


def _expert_onehots(ids_ref):
    tc = ids_ref.shape[2]
    expert = lax.broadcasted_iota(jnp.int32, (N_EXP_TOTAL, tc), 0)
    ids = ids_ref[0]
    return [jnp.where(expert == ids[k:k + 1], 1.0, 0.0) for k in range(2)]


def _rank_kernel(ids_ref, rank_o, cnt_o, carry):
    tc = ids_ref.shape[2]

    @pl.when(pl.program_id(0) == 0)
    def _():
        carry[...] = jnp.zeros_like(carry)

    r = lax.broadcasted_iota(jnp.int32, (tc, tc), 0)
    c = lax.broadcasted_iota(jnp.int32, (tc, tc), 1)
    before = jnp.where(r < c, 1.0, 0.0).astype(BF16)
    base = carry[...]
    rank_o[0] = jnp.zeros(rank_o.shape[1:], rank_o.dtype)
    for k, oh in enumerate(_expert_onehots(ids_ref)):
        prefix = _dot(oh.astype(BF16), before)
        rank_o[0, k:k + 1, :] = jnp.sum(oh * (base + prefix), axis=0, keepdims=True).astype(jnp.int32)
        base = base + jnp.sum(oh, axis=-1, keepdims=True)
    carry[...] = base
    cnt_o[...] = jnp.broadcast_to(base, cnt_o.shape)


def _pos_kernel(ids_ref, rank_ref, off_ref, pos_o):
    rank = rank_ref[0]
    pos_o[0] = jnp.zeros(pos_o.shape[1:], pos_o.dtype)
    for k, oh in enumerate(_expert_onehots(ids_ref)):
        off = jnp.sum(oh * off_ref[...], axis=0, keepdims=True).astype(jnp.int32)
        pos_o[0, k:k + 1, :] = off + rank[k:k + 1]


def _dispatch_plan(ids2):
    n = ids2.shape[0]
    nc = n // RANK_TC
    ids_t = jnp.zeros((nc, 8, RANK_TC), jnp.int32).at[:, :2].set(
        jnp.transpose(ids2.reshape(nc, RANK_TC, 2), (0, 2, 1)))
    blk = pl.BlockSpec((1, 8, RANK_TC), lambda c: (c, 0, 0))
    rank, cnt = pl.pallas_call(
        _rank_kernel,
        grid=(nc,),
        in_specs=[blk],
        out_specs=[blk, pl.BlockSpec((N_EXP_TOTAL, LANES), lambda c: (0, 0))],
        out_shape=[jax.ShapeDtypeStruct((nc, 8, RANK_TC), jnp.int32),
                   jax.ShapeDtypeStruct((N_EXP_TOTAL, LANES), F32)],
        scratch_shapes=[pltpu.VMEM((N_EXP_TOTAL, 1), F32)],
        name="moe_rank",
    )(ids_t)
    counts = cnt[:, 0].astype(jnp.int32)
    tiles_per = (counts + T_EXP - 1) // T_EXP
    tile_start = jnp.cumsum(tiles_per) - tiles_per
    n_used = jnp.sum(tiles_per)
    pos = pl.pallas_call(
        _pos_kernel,
        grid=(nc,),
        in_specs=[blk, blk, pl.BlockSpec((N_EXP_TOTAL, 1), lambda c: (0, 0))],
        out_specs=blk,
        out_shape=jax.ShapeDtypeStruct((nc, 8, RANK_TC), jnp.int32),
        name="moe_pos",
    )(ids_t, rank, (tile_start * T_EXP).astype(F32).reshape(N_EXP_TOTAL, 1))
    n_tiles = (2 * n) // T_EXP + N_EXP_TOTAL
    tile_idx = jnp.minimum(jnp.arange(n_tiles, dtype=jnp.int32), n_used - 1)
    tile_expert = jnp.sum((tile_start[None, :] <= tile_idx[:, None]).astype(jnp.int32), axis=1) - 1
    per = RANK_TC // TC
    pos_chunks = jnp.transpose(pos[:, :2].reshape(nc, 2, per, TC), (0, 2, 1, 3)).reshape(n // TC, 1, 2 * TC)
    pad_start = tile_start * T_EXP + counts
    return pos_chunks, tile_expert, n_used.reshape(1).astype(jnp.int32), n_tiles, pad_start


def _row_copy_wait(src, dst, sem, rows):
    pltpu.make_async_copy(src.at[pl.ds(0, rows)], dst.at[pl.ds(0, rows)], sem).wait()


def _dispatch_kernel(pad_ref, nu_ref, pos_ref, h_ref, xs_hbm, zero_s, sem):
    @pl.when(pl.program_id(0) == 0)
    def _():
        zero_s[...] = jnp.zeros_like(zero_s)

        def zero_block(start):
            return pltpu.make_async_copy(zero_s, xs_hbm.at[pl.ds(pl.multiple_of(start, 8), T_EXP)], sem)

        for e in range(N_EXP_TOTAL):
            zero_block((pad_ref[e] // 8) * 8).start()
        for e in range(N_EXP_TOTAL):
            zero_block(0).wait()

        def unused_tile(j, carry):
            block = zero_block(j * T_EXP)
            block.start()
            block.wait()
            return carry

        lax.fori_loop(nu_ref[0], xs_hbm.shape[0] // T_EXP, unused_tile, 0)

    def issue(i, carry):
        for k in range(2):
            dst = pos_ref[0, 0, k * TC + i]
            pltpu.make_async_copy(h_ref.at[pl.ds(i, 1)], xs_hbm.at[pl.ds(dst, 1)], sem).start()
        return carry

    lax.fori_loop(0, TC, issue, 0, unroll=ISSUE_UNROLL)
    for k in range(2):
        _row_copy_wait(h_ref, xs_hbm, sem, TC)


def _dispatch(h2_flat, pos_chunks, pad_start, n_used, n_rows):
    n, d = h2_flat.shape
    grid_spec = pltpu.PrefetchScalarGridSpec(
        num_scalar_prefetch=2,
        grid=(n // TC,),
        in_specs=[
            pl.BlockSpec((1, 1, 2 * TC), lambda c, pad, nu: (c, 0, 0), memory_space=pltpu.SMEM),
            pl.BlockSpec((TC, d), lambda c, pad, nu: (c, 0)),
        ],
        out_specs=pl.BlockSpec(memory_space=pl.ANY),
        scratch_shapes=[pltpu.VMEM((T_EXP, d), h2_flat.dtype), pltpu.SemaphoreType.DMA(())],
    )
    return pl.pallas_call(
        _dispatch_kernel,
        grid_spec=grid_spec,
        out_shape=jax.ShapeDtypeStruct((n_rows + T_EXP, d), h2_flat.dtype),
        name="moe_dispatch",
    )(pad_start, n_used, pos_chunks, h2_flat)


def _expert_kernel(te_ref, nu_ref, xs_ref, w1_ref, w3_ref, w2_ref, ys_ref, w1_s, w3_s, w2_s):
    j = pl.program_id(0)
    prev = te_ref[jnp.maximum(j - 1, 0)]

    @pl.when((j == 0) | (te_ref[j] != prev))
    def _():
        w1_s[...] = w1_ref[0].astype(BF16)
        w3_s[...] = w3_ref[0].astype(BF16)
        w2_s[...] = w2_ref[0].astype(BF16)

    @pl.when(j < nu_ref[0])
    def _():
        xb = xs_ref[...].astype(BF16)
        a = _dot(xb, w1_s[...])
        b = _dot(xb, w3_s[...])
        hid = a * jax.nn.sigmoid(a) * b
        ys_ref[...] = _dot(hid.astype(BF16), w2_s[...])

    @pl.when(j >= nu_ref[0])
    def _():
        ys_ref[...] = jnp.zeros_like(ys_ref)


def _experts(xs, n_tiles, tile_expert, n_used, w1, w3, w2, layer):
    d = xs.shape[1]
    n_rows = n_tiles * T_EXP
    f = w1.shape[-1]
    n_all = w1.shape[0] * N_EXP_TOTAL
    tile_expert = tile_expert + layer * N_EXP_TOTAL
    grid_spec = pltpu.PrefetchScalarGridSpec(
        num_scalar_prefetch=2,
        grid=(n_tiles,),
        in_specs=[
            pl.BlockSpec((T_EXP, d), lambda j, te, nu: (jnp.minimum(j, nu[0] - 1), 0)),
            pl.BlockSpec((1, d, f), lambda j, te, nu: (te[j], 0, 0)),
            pl.BlockSpec((1, d, f), lambda j, te, nu: (te[j], 0, 0)),
            pl.BlockSpec((1, f, d), lambda j, te, nu: (te[j], 0, 0)),
        ],
        out_specs=pl.BlockSpec((T_EXP, d), lambda j, te, nu: (j, 0)),
        scratch_shapes=[pltpu.VMEM((d, f), BF16), pltpu.VMEM((d, f), BF16), pltpu.VMEM((f, d), BF16)],
    )
    return pl.pallas_call(
        _expert_kernel,
        grid_spec=grid_spec,
        out_shape=jax.ShapeDtypeStruct((n_rows, d), F32),
        name="moe_experts",
    )(tile_expert, n_used, xs, w1.reshape(n_all, d, f), w3.reshape(n_all, d, f), w2.reshape(n_all, f, d))


def _combine_kernel(pos_ref, ys_hbm, x1_ref, wts_ref, mod_ref, x2_o, buf, sem):
    def issue(i, carry):
        for k in range(2):
            src = pos_ref[0, 0, k * TC + i]
            pltpu.make_async_copy(ys_hbm.at[pl.ds(src, 1)], buf.at[k, pl.ds(i, 1)], sem).start()
        return carry

    lax.fori_loop(0, TC, issue, 0, unroll=ISSUE_UNROLL)
    for k in range(2):
        _row_copy_wait(ys_hbm, buf.at[k], sem, TC)
    wts = wts_ref[0]
    y = wts[:, 0:1] * buf[0] + wts[:, 1:2] * buf[1]
    gate2 = mod_ref[0][5:6]
    x2_o[0] = x1_ref[0] + gate2 * y


def _combine(ys, pos_chunks, x1, wts, mod):
    bsz, seq, d = x1.shape
    per_b = seq // TC
    return pl.pallas_call(
        _combine_kernel,
        grid=(bsz, per_b),
        in_specs=[
            pl.BlockSpec((1, 1, 2 * TC), lambda b, s: (b * per_b + s, 0, 0), memory_space=pltpu.SMEM),
            pl.BlockSpec(memory_space=pl.ANY),
            pl.BlockSpec((1, TC, d), lambda b, s: (b, s, 0)),
            pl.BlockSpec((1, TC, LANES), lambda b, s: (b, s, 0)),
            pl.BlockSpec((1, 6, d), lambda b, s: (b, 0, 0)),
        ],
        out_specs=pl.BlockSpec((1, TC, d), lambda b, s: (b, s, 0)),
        out_shape=jax.ShapeDtypeStruct((bsz, seq, d), F32),
        scratch_shapes=[pltpu.VMEM((2, TC, d), F32), pltpu.SemaphoreType.DMA(())],
        name="moe_combine",
    )(pos_chunks, ys, x1, wts, mod)


def _moe(h2, ids, wts, x1, mod, w1, w3, w2, layer):
    bsz, seq, d = h2.shape
    n = bsz * seq
    pos_chunks, tile_expert, n_used, n_tiles, pad_start = _dispatch_plan(ids.reshape(n, LANES)[:, :2])
    xs = _dispatch(h2.reshape(n, d), pos_chunks, pad_start, n_used, n_tiles * T_EXP)
    ys = _experts(xs, n_tiles, tile_expert, n_used, w1, w3, w2, layer)
    return _combine(ys, pos_chunks, x1, wts, mod)


def kernel(x, c, positions, mod_w, mod_b, norm1_g, norm2_g, ev_w_in, ev_b_f, ev_qn_a, ev_kn_a, ev_lam,
           ev_subln_g, ev_qn_b, ev_kn_b, ev_w_out, od_w_in, od_w_out, moe_w_gr, moe_b_gr, moe_w_er,
           moe_b_er, moe_w1, moe_w3, moe_w2):
    depth = mod_w.shape[0]
    mod = _modulation(c, mod_w, mod_b)
    cos_t, sin_t = _rope_tables(positions)
    for l in range(depth):
        if l % 2 == 0:
            e = l // 2
            lambda_init = 0.8 - 0.6 * math.exp(-0.3 * l)
            qa, ka, va, qb, kb, vb = _ln_proj_even(
                x, mod[l], norm1_g[l], ev_w_in[e], ev_b_f[e], ev_qn_a[e], ev_kn_a[e], ev_qn_b[e],
                ev_kn_b[e], cos_t, sin_t)
            o = _even_mixer_attention(qa, ka, va, qb, kb, vb, ev_lam[e], ev_subln_g[e], lambda_init)
            w_out = ev_w_out[e]
        else:
            od = l // 2
            q, k, v = _ln_proj_odd(x, mod[l], norm1_g[l], od_w_in[od])
            o = _sb_attention(q, k, v)
            w_out = od_w_out[od]
        x1, h2, ids, wts = _out_proj_route(o, x, mod[l], norm2_g[l], w_out, moe_w_gr[l], moe_b_gr[l],
                                           moe_w_er[l], moe_b_er[l])
        x = _moe(h2, ids, wts, x1, mod[l], moe_w1, moe_w3, moe_w2, l)
    return x
```

```python
import functools
import math

import numpy as np
import jax
import jax.numpy as jnp
from jax import lax
from jax.experimental import pallas as pl
from jax.experimental.pallas import tpu as pltpu

F32 = jnp.float32
BF16 = jnp.bfloat16

D_MODEL = 1024
HEAD_DIM = 64
HALF = HEAD_DIM // 2
A_HEADS = 4
B_HEADS = 8
C_HEADS = 16
SEC = 512
N_GROUPS = 4
N_EXPERTS = 8
N_EXP_TOTAL = N_GROUPS * N_EXPERTS
ROPE_THETA = 10000.0
RMS_EPS = 1e-6
QK_SCALE = HEAD_DIM ** -0.5
LOG2E = math.log2(math.e)

LANES = 128
TM = 512
TQ_SOFTMAX = 1024
TQ_STICK = 1024
TK_SOFTMAX = 1024
TK_STICK = 512
SB_BLOCK = 256
FOX_AUG = 16
FOX_K = HEAD_DIM + FOX_AUG
T_EXP = 256
TC = 1024
RANK_TC = 1024
ISSUE_UNROLL = 8
NEG = -1e30
ROUTE_E0 = 8

NT_DIMS = (((1,), (1,)), ((), ()))
TN_DIMS = (((0,), (0,)), ((), ()))


def _dot(a, b):
    return jnp.dot(a, b, preferred_element_type=F32)


def _dot_nt(a, b):
    return lax.dot_general(a, b, NT_DIMS, preferred_element_type=F32)


def _dot_tn(a, b):
    return lax.dot_general(a, b, TN_DIMS, preferred_element_type=F32)


def _split2(x):
    hi = x.astype(BF16)
    lo = (x - hi.astype(F32)).astype(BF16)
    return hi, lo


def _split3(x):
    hi = x.astype(BF16)
    r = x - hi.astype(F32)
    mid = r.astype(BF16)
    lo = (r - mid.astype(F32)).astype(BF16)
    return hi, mid, lo


def _softplus_neg_abs(z):
    return jnp.log(1.0 + jnp.exp(-jnp.abs(z)))


def _log_sigmoid(z):
    return jnp.minimum(z, 0.0) - _softplus_neg_abs(z)


def _rms_rows(x, eps=RMS_EPS):
    return x * lax.rsqrt(jnp.mean(x * x, axis=-1, keepdims=True) + eps)


def _mod_kernel(c_ref, w_ref, b_ref, o_ref):
    c = c_ref[...]
    ca = c * jax.nn.sigmoid(c)
    c_hi, c_mid, c_lo = _split3(ca)
    w = w_ref[0]
    w_hi, w_lo = _split2(w)
    acc = _dot(c_hi, w_hi) + _dot(c_hi, w_lo) + _dot(c_mid, w_hi) + _dot(c_lo, w_hi) + _dot(c_mid, w_lo)
    o_ref[0] = acc + b_ref[0]


def _modulation(c, mod_w, mod_b):
    depth, d, n6 = mod_w.shape
    bsz = c.shape[0]
    rows = 8
    tn = 1536
    c_pad = jnp.zeros((rows, d), F32).at[:bsz].set(c)
    out = pl.pallas_call(
        _mod_kernel,
        grid=(depth, n6 // tn),
        in_specs=[
            pl.BlockSpec((rows, d), lambda l, j: (0, 0)),
            pl.BlockSpec((1, d, tn), lambda l, j: (l, 0, j)),
            pl.BlockSpec((1, 1, tn), lambda l, j: (l, 0, j)),
        ],
        out_specs=pl.BlockSpec((1, rows, tn), lambda l, j: (l, 0, j)),
        out_shape=jax.ShapeDtypeStruct((depth, rows, n6), F32),
        name="adaln_mod",
    )(c_pad, mod_w, mod_b.reshape(depth, 1, n6))
    return out[:, :bsz].reshape(depth, bsz, 6, d)


def _rope_table_kernel(pos_ref, inv_ref, cos_ref, sin_ref):
    ang = pos_ref[0].astype(F32) * inv_ref[...]
    cos_ref[0] = jnp.cos(ang)
    sin_ref[0] = jnp.sin(ang)


def _rope_tables(positions):
    bsz, seq = positions.shape
    ts = 2048
    inv = ROPE_THETA ** (-2.0 * jnp.arange(HALF, dtype=F32) / HEAD_DIM)
    return pl.pallas_call(
        _rope_table_kernel,
        grid=(bsz, seq // ts),
        in_specs=[
            pl.BlockSpec((1, 1, ts), lambda b, s: (b, 0, s)),
            pl.BlockSpec((HALF, 1), lambda b, s: (0, 0)),
        ],
        out_specs=[pl.BlockSpec((1, HALF, ts), lambda b, s: (b, 0, s))] * 2,
        out_shape=[jax.ShapeDtypeStruct((bsz, HALF, seq), F32)] * 2,
        name="rope_tables",
    )(positions.reshape(bsz, 1, seq), inv.reshape(HALF, 1))


def _adaln(x, mod_rows, g, first):
    shift = mod_rows[first:first + 1]
    scale = mod_rows[first + 1:first + 2]
    return _rms_rows(x) * g * (1.0 + scale) + shift


def _ln_proj_even_kernel(x_ref, mod_ref, g_ref, wT_ref, wfT_ref, bf_ref,
                         qna_ref, kna_ref, qnb_ref, knb_ref, cos_ref, sin_ref,
                         qa_o, ka_o, va_o, qb_o, kb_o, vb_o, carry):
    tm = x_ref.shape[1]

    @pl.when(pl.program_id(1) == 0)
    def _():
        carry[...] = jnp.zeros_like(carry)

    hb = _adaln(x_ref[0], mod_ref[0], g_ref[...], 0).astype(BF16)
    cos = cos_ref[0]
    sin = sin_ref[0]

    def section(idx):
        return _dot_nt(wT_ref[idx * SEC:(idx + 1) * SEC, :], hb)

    def norm_heads(p, g_col, out_ref, rope, scale, stride):
        for j in range(SEC // HEAD_DIM):
            xj = p[j * HEAD_DIM:(j + 1) * HEAD_DIM]
            yj = xj * lax.rsqrt(jnp.mean(xj * xj, axis=0, keepdims=True) + RMS_EPS) * g_col
            if rope:
                y1 = yj[:HALF]
                y2 = yj[HALF:]
                yj = jnp.concatenate([y1 * cos - y2 * sin, y2 * cos + y1 * sin], axis=0)
            out_ref[0, j * stride:j * stride + HEAD_DIM, :] = (yj * scale).astype(out_ref.dtype)

    softmax_q_scale = QK_SCALE * LOG2E
    norm_heads(section(0), qna_ref[...], qa_o, True, softmax_q_scale, HEAD_DIM)
    norm_heads(section(1), kna_ref[...], ka_o, True, 1.0, HEAD_DIM)
    va_o[0] = section(2).astype(va_o.dtype)
    norm_heads(section(3), qnb_ref[...], qb_o, False, softmax_q_scale, FOX_K)
    norm_heads(section(4), knb_ref[...], kb_o, False, 1.0, FOX_K)
    vb_o[0] = section(5).astype(vb_o.dtype)

    r = lax.broadcasted_iota(jnp.int32, (tm, tm), 0)
    c = lax.broadcasted_iota(jnp.int32, (tm, tm), 1)
    upto = jnp.where(r <= c, 1.0, 0.0).astype(BF16)
    log_f = _log_sigmoid(_dot_nt(wfT_ref[...], hb)[:B_HEADS] + bf_ref[...])
    f_hi, f_mid, f_lo = _split3(log_f)
    cum = _dot(f_hi, upto) + _dot(f_mid, upto) + _dot(f_lo, upto) + carry[...]
    carry[...] = cum[:, tm - 1:tm]

    row = lax.broadcasted_iota(jnp.int32, (FOX_AUG, tm), 0)
    for j in range(B_HEADS):
        pieces = [p.astype(F32) for p in _split3(cum[j:j + 1] * LOG2E)]
        q_aug = jnp.where(row < 3, 1.0, 0.0)
        k_aug = jnp.where((row >= 3) & (row < 6), 1.0, 0.0)
        for i, piece in enumerate(pieces):
            q_aug = jnp.where(row == 3 + i, piece, q_aug)
            k_aug = jnp.where(row == i, -piece, k_aug)
        lo = j * FOX_K + HEAD_DIM
        qb_o[0, lo:lo + FOX_AUG, :] = q_aug.astype(qb_o.dtype)
        kb_o[0, lo:lo + FOX_AUG, :] = k_aug.astype(kb_o.dtype)


def _ln_proj_even(x, mod, g, w_in, b_f, qn_a, kn_a, qn_b, kn_b, cos_t, sin_t):
    bsz, seq, d = x.shape
    n_main = 6 * SEC
    wT = w_in[:, :n_main].T.astype(BF16)
    wfT = jnp.zeros((16, d), F32).at[:B_HEADS].set(w_in[:, n_main:].T).astype(BF16)
    col = lambda v: v.reshape(HEAD_DIM, 1)
    full = lambda shape: pl.BlockSpec(shape, lambda b, s: (0,) * len(shape))
    rows_spec = lambda rows: pl.BlockSpec((1, rows, TM), lambda b, s: (b, 0, s))
    rows_shape = lambda rows: jax.ShapeDtypeStruct((bsz, rows, seq), BF16)
    out_rows = [SEC, SEC, SEC, B_HEADS * FOX_K, B_HEADS * FOX_K, SEC]
    return pl.pallas_call(
        _ln_proj_even_kernel,
        grid=(bsz, seq // TM),
        in_specs=[
            pl.BlockSpec((1, TM, d), lambda b, s: (b, s, 0)),
            pl.BlockSpec((1, 6, d), lambda b, s: (b, 0, 0)),
            full((1, d)), full((n_main, d)), full((16, d)), full((B_HEADS, 1)),
            full((HEAD_DIM, 1)), full((HEAD_DIM, 1)), full((HEAD_DIM, 1)), full((HEAD_DIM, 1)),
            rows_spec(HALF), rows_spec(HALF),
        ],
        out_specs=[rows_spec(n) for n in out_rows],
        out_shape=[rows_shape(n) for n in out_rows],
        scratch_shapes=[pltpu.VMEM((B_HEADS, 1), F32)],
        name="ln_proj_even",
    )(x, mod, g.reshape(1, d), wT, wfT, b_f.reshape(B_HEADS, 1), col(qn_a), col(kn_a), col(qn_b),
      col(kn_b), cos_t, sin_t)


def _ln_proj_odd_kernel(x_ref, mod_ref, g_ref, wT_ref, q_o, k_o, v_o):
    hb = _adaln(x_ref[0], mod_ref[0], g_ref[...], 0).astype(BF16)
    width = q_o.shape[1]
    for idx, (out_ref, scale) in enumerate(((q_o, QK_SCALE * LOG2E), (k_o, 1.0), (v_o, 1.0))):
        for half in range(2):
            lo = idx * width + half * (width // 2)
            p = _dot_nt(wT_ref[lo:lo + width // 2, :], hb)
            out_ref[0, half * (width // 2):(half + 1) * (width // 2), :] = (p * scale).astype(out_ref.dtype)


def _ln_proj_odd(x, mod, g, w_in):
    bsz, seq, d = x.shape
    width = C_HEADS * HEAD_DIM
    wT = w_in.T.astype(BF16)
    full = lambda shape: pl.BlockSpec(shape, lambda b, s: (0,) * len(shape))
    spec = pl.BlockSpec((1, width, TM), lambda b, s: (b, 0, s))
    shape = jax.ShapeDtypeStruct((bsz, width, seq), BF16)
    return pl.pallas_call(
        _ln_proj_odd_kernel,
        grid=(bsz, seq // TM),
        in_specs=[
            pl.BlockSpec((1, TM, d), lambda b, s: (b, s, 0)),
            pl.BlockSpec((1, 6, d), lambda b, s: (b, 0, 0)),
            full((1, d)), full((3 * width, d)),
        ],
        out_specs=[spec] * 3,
        out_shape=[shape] * 3,
        name="ln_proj_odd",
    )(x, mod, g.reshape(1, d), wT)


def _diag_valid(shape, strict):
    key = lax.broadcasted_iota(jnp.int32, shape, 0)
    qry = lax.broadcasted_iota(jnp.int32, shape, 1)
    return (qry > key) if strict else (qry >= key)


def _query_cols(diag):
    return slice(0 if diag is None else diag, None)


def _softmax_tile(kT, qT_ref, q_rows, vT, m_ref, l_ref, acc_ref, diag):
    qs = _query_cols(diag)
    sT = _dot_tn(kT, qT_ref[0, q_rows, qs])
    if diag is not None:
        sT = jnp.where(_diag_valid(sT.shape, False), sT, NEG)
    m_prev = m_ref[:, qs]
    m_new = jnp.maximum(m_prev, jnp.max(sT, axis=0, keepdims=True))
    alpha = jnp.exp2(m_prev - m_new)
    p = jnp.exp2(sT - m_new)
    l_ref[:, qs] = alpha * l_ref[:, qs] + jnp.sum(p, axis=0, keepdims=True)
    acc_ref[:, qs] = alpha * acc_ref[:, qs] + _dot(vT, p.astype(BF16))
    m_ref[:, qs] = m_new


def _init_softmax(m_refs, l_refs, acc_refs):
    for r in m_refs:
        r[...] = jnp.full_like(r, -jnp.inf)
    for r in l_refs + acc_refs:
        r[...] = jnp.zeros_like(r)


def _key_tile(ki, tk):
    return pl.ds(pl.multiple_of(ki * tk, tk), tk)


def _attn_diff_kernel(qT_ref, kT_ref, vT_ref, lam_ref, sg_ref, o_ref,
                      m0, m1, l0, l1, acc0, acc1, *, lambda_init):
    qi = pl.program_id(2)
    ms, ls, accs = [m0, m1], [l0, l1], [acc0, acc1]
    _init_softmax(ms, ls, accs)

    def tiles(ki, diag):
        cols = _key_tile(ki, TK_SOFTMAX)
        for c in range(2):
            hs = slice(c * HEAD_DIM, (c + 1) * HEAD_DIM)
            _softmax_tile(kT_ref[0, hs, cols], qT_ref, hs, vT_ref[0, :, cols], ms[c], ls[c], accs[c], diag)

    def below_diagonal(ki, carry):
        tiles(ki, None)
        return carry

    diag_tiles = qT_ref.shape[2] // TK_SOFTMAX
    n_below = qi * diag_tiles
    lax.fori_loop(0, n_below, below_diagonal, 0)
    for d in range(diag_tiles):
        tiles(n_below + d, d * TK_SOFTMAX)
    lam = lam_ref[...]
    e1 = jnp.exp(jnp.sum(lam[0:1] * lam[1:2], axis=-1, keepdims=True))
    e2 = jnp.exp(jnp.sum(lam[2:3] * lam[3:4], axis=-1, keepdims=True))
    lam_full = e1 - e2 + lambda_init
    oT = acc0[...] * (1.0 / l0[...]) - lam_full * (acc1[...] * (1.0 / l1[...]))
    oT = oT * lax.rsqrt(jnp.mean(oT * oT, axis=0, keepdims=True) + RMS_EPS)
    oT = oT * sg_ref[...] * (1.0 - lambda_init)
    o_ref[0] = oT.T.astype(o_ref.dtype)


def _attn_fox_kernel(qT_ref, kT_ref, vT_ref, o_ref, m0, m1, l0, l1, acc0, acc1):
    qi = pl.program_id(2)
    ms, ls, accs = [m0, m1], [l0, l1], [acc0, acc1]
    _init_softmax(ms, ls, accs)

    def tiles(ki, diag):
        cols = _key_tile(ki, TK_SOFTMAX)
        for j in range(2):
            ks = slice(j * FOX_K, (j + 1) * FOX_K)
            vs = slice(j * HEAD_DIM, (j + 1) * HEAD_DIM)
            _softmax_tile(kT_ref[0, ks, cols], qT_ref, ks, vT_ref[0, vs, cols], ms[j], ls[j], accs[j], diag)

    def below_diagonal(ki, carry):
        tiles(ki, None)
        return carry

    diag_tiles = qT_ref.shape[2] // TK_SOFTMAX
    n_below = qi * diag_tiles
    lax.fori_loop(0, n_below, below_diagonal, 0)
    for d in range(diag_tiles):
        tiles(n_below + d, d * TK_SOFTMAX)
    oT = jnp.concatenate([acc0[...] * (1.0 / l0[...]), acc1[...] * (1.0 / l1[...])], axis=0)
    o_ref[0] = oT.T.astype(o_ref.dtype)


def _sb_tile(kT, qT_ref, q_rows, vT, later_ref, r_ref, acc_ref, diag):
    qs = _query_cols(diag)
    zT = _dot_tn(kT, qT_ref[0, q_rows, qs])
    neg_keep = jnp.maximum(zT, 0.0) + jnp.log2(1.0 + jnp.exp2(-jnp.abs(zT)))
    log_beta = zT - neg_keep
    if diag is not None:
        valid = _diag_valid(zT.shape, True)
        neg_keep = jnp.where(valid, neg_keep, 0.0)
    keep_bf = neg_keep.astype(BF16)
    beyond = r_ref[:, qs]
    parts = []
    for blk in reversed(range(zT.shape[0] // SB_BLOCK)):
        rows = slice(blk * SB_BLOCK, (blk + 1) * SB_BLOCK)
        within = _dot(later_ref[...], keep_bf[rows])
        parts.append(within + beyond)
        beyond = beyond + within[0:1] + neg_keep[blk * SB_BLOCK:blk * SB_BLOCK + 1]
    laterT = jnp.concatenate(parts[::-1], axis=0)
    a = jnp.exp2(log_beta - laterT)
    if diag is not None:
        a = jnp.where(valid, a, 0.0)
    acc_ref[:, qs] = acc_ref[:, qs] + _dot(vT, a.astype(BF16))
    r_ref[:, qs] = beyond


def _attn_sb_kernel(qT_ref, kT_ref, vT_ref, later_ref, o_ref, r0, r1, acc0, acc1):
    qi = pl.program_id(2)
    rs, accs = [r0, r1], [acc0, acc1]
    for r in rs + accs:
        r[...] = jnp.zeros_like(r)

    def tiles(ki, diag):
        cols = _key_tile(ki, TK_STICK)
        for j in range(2):
            hs = slice(j * HEAD_DIM, (j + 1) * HEAD_DIM)
            _sb_tile(kT_ref[0, hs, cols], qT_ref, hs, vT_ref[0, hs, cols], later_ref, rs[j], accs[j], diag)

    diag_tiles = qT_ref.shape[2] // TK_STICK
    n_below = qi * diag_tiles

    def below_diagonal(i, carry):
        tiles(n_below - 1 - i, None)
        return carry

    for d in reversed(range(diag_tiles)):
        tiles(n_below + d, d * TK_STICK)
    lax.fori_loop(0, n_below, below_diagonal, 0)
    o_ref[0] = jnp.concatenate([acc0[...], acc1[...]], axis=0).T.astype(o_ref.dtype)


def _attention_call(body, name, n_groups, tq, qT, kT, vT, qk_rows, v_rows, extra_in, extra_specs, n_stats,
                    v_dim):
    bsz, _, seq = qT.shape
    in_specs = [
        pl.BlockSpec((1, qk_rows, tq), lambda b, h, qi: (b, h, qi)),
        pl.BlockSpec((1, qk_rows, seq), lambda b, h, qi: (b, h, 0)),
        pl.BlockSpec((1, v_rows, seq), lambda b, h, qi: (b, h, 0)),
    ] + extra_specs
    scratch = [pltpu.VMEM((1, tq), F32)] * n_stats + [pltpu.VMEM((v_dim, tq), F32)] * 2
    return pl.pallas_call(
        body,
        grid=(bsz, n_groups, seq // tq),
        in_specs=in_specs,
        out_specs=pl.BlockSpec((1, tq, 2 * HEAD_DIM), lambda b, h, qi: (b, qi, h)),
        out_shape=jax.ShapeDtypeStruct((bsz, seq, n_groups * 2 * HEAD_DIM), BF16),
        scratch_shapes=scratch,
        name=name,
    )(qT, kT, vT, *extra_in)


def _even_mixer_attention(qa, ka, va, qb, kb, vb, lam, subln_g, lambda_init):
    full = lambda shape: pl.BlockSpec(shape, lambda b, h, qi: (0,) * len(shape))
    pair = 2 * HEAD_DIM
    o_a = _attention_call(
        functools.partial(_attn_diff_kernel, lambda_init=lambda_init), "attn_diff", A_HEADS, TQ_SOFTMAX,
        qa, ka, va, pair, pair, [lam, subln_g.reshape(pair, 1)],
        [full((4, HEAD_DIM)), full((pair, 1))], 4, pair)
    o_b = _attention_call(_attn_fox_kernel, "attn_fox", B_HEADS // 2, TQ_SOFTMAX, qb, kb, vb, 2 * FOX_K, pair,
                          [], [], 4, HEAD_DIM)
    return [o_a, o_b]


def _sb_attention(qT, kT, vT):
    full = lambda shape: pl.BlockSpec(shape, lambda b, h, qi: (0,) * len(shape))
    pair = 2 * HEAD_DIM
    later = jnp.triu(jnp.ones((SB_BLOCK, SB_BLOCK), F32), 1).astype(BF16)
    return [_attention_call(_attn_sb_kernel, "attn_sb", C_HEADS // 2, TQ_STICK, qT, kT, vT, pair, pair,
                            [later], [full((SB_BLOCK, SB_BLOCK))], 2, HEAD_DIM)]


def _out_proj_route_kernel(*refs, n_parts):
    o_refs = refs[:n_parts]
    wo_refs = refs[n_parts:2 * n_parts]
    x_ref, mod_ref, g_ref, wrh_ref, wrl_ref, br_ref, x1_o, h2_o, ids_o, wts_o = refs[2 * n_parts:]
    mod_rows = mod_ref[0]
    gate1 = mod_rows[2:3]
    mix = _dot(o_refs[0][0], wo_refs[0][...])
    for o_ref, wo_ref in zip(o_refs[1:], wo_refs[1:]):
        mix = mix + _dot(o_ref[0], wo_ref[...])
    x1 = x_ref[0] + gate1 * mix
    x1_o[0] = x1
    h2 = _adaln(x1, mod_rows, g_ref[...], 3)
    h2_o[0] = h2

    h_hi, h_lo = _split2(h2)
    wr_hi = wrh_ref[...]
    logits = _dot(h_hi, wr_hi) + _dot(h_hi, wrl_ref[...]) + _dot(h_lo, wr_hi) + br_ref[...]

    lane = lax.broadcasted_iota(jnp.int32, logits.shape, 1)
    big = jnp.int32(LANES)
    g_mask = lane < N_GROUPS
    g_log = jnp.where(g_mask, logits, NEG)
    g_max = jnp.max(g_log, axis=-1, keepdims=True)
    g_sum = jnp.sum(jnp.where(g_mask, jnp.exp(g_log - g_max), 0.0), axis=-1, keepdims=True)
    g_w = 1.0 / g_sum
    g_idx = jnp.min(jnp.where(g_mask & (g_log == g_max), lane, big), axis=-1, keepdims=True)

    e_mask = (lane >= ROUTE_E0) & (lane < ROUTE_E0 + N_EXP_TOTAL) & ((lane >> 3) == g_idx + 1)
    e_log = jnp.where(e_mask, logits, NEG)
    e_max = jnp.max(e_log, axis=-1, keepdims=True)
    e_exp = jnp.where(e_mask, jnp.exp(e_log - e_max), 0.0)
    e_prob = e_exp / jnp.sum(e_exp, axis=-1, keepdims=True)
    p1 = jnp.max(jnp.where(e_mask, e_prob, -1.0), axis=-1, keepdims=True)
    i1 = jnp.min(jnp.where(e_mask & (e_prob == p1), lane, big), axis=-1, keepdims=True)
    rest = e_mask & (lane != i1)
    p2 = jnp.max(jnp.where(rest, e_prob, -1.0), axis=-1, keepdims=True)
    i2 = jnp.min(jnp.where(rest & (e_prob == p2), lane, big), axis=-1, keepdims=True)
    den = p1 + p2
    w1 = p1 / den * g_w
    w2 = p2 / den * g_w
    ids_o[0] = jnp.where(lane == 0, i1 - ROUTE_E0, jnp.where(lane == 1, i2 - ROUTE_E0, 0))
    wts_o[0] = jnp.where(lane == 0, w1, jnp.where(lane == 1, w2, 0.0))


def _out_proj_route(o_parts, x, mod, g2, w_out, w_gr, b_gr, w_er, b_er):
    bsz, seq, d = x.shape
    widths = [o.shape[-1] for o in o_parts]
    starts = np.cumsum([0] + widths)
    wo_parts = [w_out[starts[i]:starts[i + 1]].astype(BF16) for i in range(len(o_parts))]
    wr = jnp.zeros((d, LANES), F32)
    wr = wr.at[:, :N_GROUPS].set(w_gr)
    wr = wr.at[:, ROUTE_E0:ROUTE_E0 + N_EXP_TOTAL].set(
        jnp.transpose(w_er, (1, 0, 2)).reshape(d, N_EXP_TOTAL))
    br = jnp.zeros((1, LANES), F32)
    br = br.at[0, :N_GROUPS].set(b_gr)
    br = br.at[0, ROUTE_E0:ROUTE_E0 + N_EXP_TOTAL].set(b_er.reshape(N_EXP_TOTAL))
    wr_hi = wr.astype(BF16)
    wr_lo = (wr - wr_hi.astype(F32)).astype(BF16)
    full = lambda shape: pl.BlockSpec(shape, lambda b, s: (0,) * len(shape))
    tile = lambda w: pl.BlockSpec((1, TM, w), lambda b, s: (b, s, 0))
    return pl.pallas_call(
        functools.partial(_out_proj_route_kernel, n_parts=len(o_parts)),
        grid=(bsz, seq // TM),
        in_specs=[tile(w) for w in widths] + [full((w, d)) for w in widths] + [
            tile(d),
            pl.BlockSpec((1, 6, d), lambda b, s: (b, 0, 0)),
            full((1, d)), full((d, LANES)), full((d, LANES)), full((1, LANES)),
        ],
        out_specs=[tile(d), tile(d), tile(LANES), tile(LANES)],
        out_shape=[
            jax.ShapeDtypeStruct((bsz, seq, d), F32),
            jax.ShapeDtypeStruct((bsz, seq, d), F32),
            jax.ShapeDtypeStruct((bsz, seq, LANES), jnp.int32),
            jax.ShapeDtypeStruct((bsz, seq, LANES), F32),
        ],
        name="out_proj_route",
    )(*o_parts, *wo_parts, x, mod, g2.reshape(1, d), wr_hi, wr_lo, br)


def _expert_onehots(ids_ref):
    tc = ids_ref.shape[2]
    expert = lax.broadcasted_iota(jnp.int32, (N_EXP_TOTAL, tc), 0)
    ids = ids_ref[0]
    return [jnp.where(expert == ids[k:k + 1], 1.0, 0.0) for k in range(2)]


def _rank_kernel(ids_ref, rank_o, cnt_o, carry):
    tc = ids_ref.shape[2]

    @pl.when(pl.program_id(0) == 0)
    def _():
        carry[...] = jnp.zeros_like(carry)

    r = lax.broadcasted_iota(jnp.int32, (tc, tc), 0)
    c = lax.broadcasted_iota(jnp.int32, (tc, tc), 1)
    before = jnp.where(r < c, 1.0, 0.0).astype(BF16)
    base = carry[...]
    rank_o[0] = jnp.zeros(rank_o.shape[1:], rank_o.dtype)
    for k, oh in enumerate(_expert_onehots(ids_ref)):
        prefix = _dot(oh.astype(BF16), before)
        rank_o[0, k:k + 1, :] = jnp.sum(oh * (base + prefix), axis=0, keepdims=True).astype(jnp.int32)
        base = base + jnp.sum(oh, axis=-1, keepdims=True)
    carry[...] = base
    cnt_o[...] = jnp.broadcast_to(base, cnt_o.shape)


def _pos_kernel(ids_ref, rank_ref, off_ref, pos_o):
    rank = rank_ref[0]
    pos_o[0] = jnp.zeros(pos_o.shape[1:], pos_o.dtype)
    for k, oh in enumerate(_expert_onehots(ids_ref)):
        off = jnp.sum(oh * off_ref[...], axis=0, keepdims=True).astype(jnp.int32)
        pos_o[0, k:k + 1, :] = off + rank[k:k + 1]


def _dispatch_plan(ids2):
    n = ids2.shape[0]
    nc = n // RANK_TC
    ids_t = jnp.zeros((nc, 8, RANK_TC), jnp.int32).at[:, :2].set(
        jnp.transpose(ids2.reshape(nc, RANK_TC, 2), (0, 2, 1)))
    blk = pl.BlockSpec((1, 8, RANK_TC), lambda c: (c, 0, 0))
    rank, cnt = pl.pallas_call(
        _rank_kernel,
        grid=(nc,),
        in_specs=[blk],
        out_specs=[blk, pl.BlockSpec((N_EXP_TOTAL, LANES), lambda c: (0, 0))],
        out_shape=[jax.ShapeDtypeStruct((nc, 8, RANK_TC), jnp.int32),
                   jax.ShapeDtypeStruct((N_EXP_TOTAL, LANES), F32)],
        scratch_shapes=[pltpu.VMEM((N_EXP_TOTAL, 1), F32)],
        name="moe_rank",
    )(ids_t)
    counts = cnt[:, 0].astype(jnp.int32)
    tiles_per = (counts + T_EXP - 1) // T_EXP
    tile_start = jnp.cumsum(tiles_per) - tiles_per
    n_used = jnp.sum(tiles_per)
    pos = pl.pallas_call(
        _pos_kernel,
        grid=(nc,),
        in_specs=[blk, blk, pl.BlockSpec((N_EXP_TOTAL, 1), lambda c: (0, 0))],
        out_specs=blk,
        out_shape=jax.ShapeDtypeStruct((nc, 8, RANK_TC), jnp.int32),
        name="moe_pos",
    )(ids_t, rank, (tile_start * T_EXP).astype(F32).reshape(N_EXP_TOTAL, 1))
    n_tiles = (2 * n) // T_EXP + N_EXP_TOTAL
    tile_idx = jnp.minimum(jnp.arange(n_tiles, dtype=jnp.int32), n_used - 1)
    tile_expert = jnp.sum((tile_start[None, :] <= tile_idx[:, None]).astype(jnp.int32), axis=1) - 1
    per = RANK_TC // TC
    pos_chunks = jnp.transpose(pos[:, :2].reshape(nc, 2, per, TC), (0, 2, 1, 3)).reshape(n // TC, 1, 2 * TC)
    pad_start = tile_start * T_EXP + counts
    return pos_chunks, tile_expert, n_used.reshape(1).astype(jnp.int32), n_tiles, pad_start


def _row_copy_wait(src, dst, sem, rows):
    pltpu.make_async_copy(src.at[pl.ds(0, rows)], dst.at[pl.ds(0, rows)], sem).wait()


def _dispatch_kernel(pad_ref, nu_ref, pos_ref, h_ref, xs_hbm, zero_s, sem):
    @pl.when(pl.program_id(0) == 0)
    def _():
        zero_s[...] = jnp.zeros_like(zero_s)

        def zero_block(start):
            return pltpu.make_async_copy(zero_s, xs_hbm.at[pl.ds(pl.multiple_of(start, 8), T_EXP)], sem)

        for e in range(N_EXP_TOTAL):
            zero_block((pad_ref[e] // 8) * 8).start()
        for e in range(N_EXP_TOTAL):
            zero_block(0).wait()

        def unused_tile(j, carry):
            block = zero_block(j * T_EXP)
            block.start()
            block.wait()
            return carry

        lax.fori_loop(nu_ref[0], xs_hbm.shape[0] // T_EXP, unused_tile, 0)

    def issue(i, carry):
        for k in range(2):
            dst = pos_ref[0, 0, k * TC + i]
            pltpu.make_async_copy(h_ref.at[pl.ds(i, 1)], xs_hbm.at[pl.ds(dst, 1)], sem).start()
        return carry

    lax.fori_loop(0, TC, issue, 0, unroll=ISSUE_UNROLL)
    for k in range(2):
        _row_copy_wait(h_ref, xs_hbm, sem, TC)


def _dispatch(h2_flat, pos_chunks, pad_start, n_used, n_rows):
    n, d = h2_flat.shape
    grid_spec = pltpu.PrefetchScalarGridSpec(
        num_scalar_prefetch=2,
        grid=(n // TC,),
        in_specs=[
            pl.BlockSpec((1, 1, 2 * TC), lambda c, pad, nu: (c, 0, 0), memory_space=pltpu.SMEM),
            pl.BlockSpec((TC, d), lambda c, pad, nu: (c, 0)),
        ],
        out_specs=pl.BlockSpec(memory_space=pl.ANY),
        scratch_shapes=[pltpu.VMEM((T_EXP, d), h2_flat.dtype), pltpu.SemaphoreType.DMA(())],
    )
    return pl.pallas_call(
        _dispatch_kernel,
        grid_spec=grid_spec,
        out_shape=jax.ShapeDtypeStruct((n_rows + T_EXP, d), h2_flat.dtype),
        name="moe_dispatch",
    )(pad_start, n_used, pos_chunks, h2_flat)


def _expert_kernel(te_ref, nu_ref, xs_ref, w1_ref, w3_ref, w2_ref, ys_ref, w1_s, w3_s, w2_s):
    j = pl.program_id(0)
    prev = te_ref[jnp.maximum(j - 1, 0)]

    @pl.when((j == 0) | (te_ref[j] != prev))
    def _():
        w1_s[...] = w1_ref[0].astype(BF16)
        w3_s[...] = w3_ref[0].astype(BF16)
        w2_s[...] = w2_ref[0].astype(BF16)

    @pl.when(j < nu_ref[0])
    def _():
        xb = xs_ref[...].astype(BF16)
        a = _dot(xb, w1_s[...])
        b = _dot(xb, w3_s[...])
        hid = a * jax.nn.sigmoid(a) * b
        ys_ref[...] = _dot(hid.astype(BF16), w2_s[...])

    @pl.when(j >= nu_ref[0])
    def _():
        ys_ref[...] = jnp.zeros_like(ys_ref)


def _experts(xs, n_tiles, tile_expert, n_used, w1, w3, w2, layer):
    d = xs.shape[1]
    n_rows = n_tiles * T_EXP
    f = w1.shape[-1]
    n_all = w1.shape[0] * N_EXP_TOTAL
    tile_expert = tile_expert + layer * N_EXP_TOTAL
    grid_spec = pltpu.PrefetchScalarGridSpec(
        num_scalar_prefetch=2,
        grid=(n_tiles,),
        in_specs=[
            pl.BlockSpec((T_EXP, d), lambda j, te, nu: (jnp.minimum(j, nu[0] - 1), 0)),
            pl.BlockSpec((1, d, f), lambda j, te, nu: (te[j], 0, 0)),
            pl.BlockSpec((1, d, f), lambda j, te, nu: (te[j], 0, 0)),
            pl.BlockSpec((1, f, d), lambda j, te, nu: (te[j], 0, 0)),
        ],
        out_specs=pl.BlockSpec((T_EXP, d), lambda j, te, nu: (j, 0)),
        scratch_shapes=[pltpu.VMEM((d, f), BF16), pltpu.VMEM((d, f), BF16), pltpu.VMEM((f, d), BF16)],
    )
    return pl.pallas_call(
        _expert_kernel,
        grid_spec=grid_spec,
        out_shape=jax.ShapeDtypeStruct((n_rows, d), F32),
        name="moe_experts",
    )(tile_expert, n_used, xs, w1.reshape(n_all, d, f), w3.reshape(n_all, d, f), w2.reshape(n_all, f, d))


def _combine_kernel(pos_ref, ys_hbm, x1_ref, wts_ref, mod_ref, x2_o, buf, sem):
    def issue(i, carry):
        for k in range(2):
            src = pos_ref[0, 0, k * TC + i]
            pltpu.make_async_copy(ys_hbm.at[pl.ds(src, 1)], buf.at[k, pl.ds(i, 1)], sem).start()
        return carry

    lax.fori_loop(0, TC, issue, 0, unroll=ISSUE_UNROLL)
    for k in range(2):
        _row_copy_wait(ys_hbm, buf.at[k], sem, TC)
    wts = wts_ref[0]
    y = wts[:, 0:1] * buf[0] + wts[:, 1:2] * buf[1]
    gate2 = mod_ref[0][5:6]
    x2_o[0] = x1_ref[0] + gate2 * y


def _combine(ys, pos_chunks, x1, wts, mod):
    bsz, seq, d = x1.shape
    per_b = seq // TC
    return pl.pallas_call(
        _combine_kernel,
        grid=(bsz, per_b),
        in_specs=[
            pl.BlockSpec((1, 1, 2 * TC), lambda b, s: (b * per_b + s, 0, 0), memory_space=pltpu.SMEM),
            pl.BlockSpec(memory_space=pl.ANY),
            pl.BlockSpec((1, TC, d), lambda b, s: (b, s, 0)),
            pl.BlockSpec((1, TC, LANES), lambda b, s: (b, s, 0)),
            pl.BlockSpec((1, 6, d), lambda b, s: (b, 0, 0)),
        ],
        out_specs=pl.BlockSpec((1, TC, d), lambda b, s: (b, s, 0)),
        out_shape=jax.ShapeDtypeStruct((bsz, seq, d), F32),
        scratch_shapes=[pltpu.VMEM((2, TC, d), F32), pltpu.SemaphoreType.DMA(())],
        name="moe_combine",
    )(pos_chunks, ys, x1, wts, mod)


def _moe(h2, ids, wts, x1, mod, w1, w3, w2, layer):
    bsz, seq, d = h2.shape
    n = bsz * seq
    pos_chunks, tile_expert, n_used, n_tiles, pad_start = _dispatch_plan(ids.reshape(n, LANES)[:, :2])
    xs = _dispatch(h2.reshape(n, d), pos_chunks, pad_start, n_used, n_tiles * T_EXP)
    ys = _experts(xs, n_tiles, tile_expert, n_used, w1, w3, w2, layer)
    return _combine(ys, pos_chunks, x1, wts, mod)


def kernel(x, c, positions, mod_w, mod_b, norm1_g, norm2_g, ev_w_in, ev_b_f, ev_qn_a, ev_kn_a, ev_lam,
           ev_subln_g, ev_qn_b, ev_kn_b, ev_w_out, od_w_in, od_w_out, moe_w_gr, moe_b_gr, moe_w_er,
           moe_b_er, moe_w1, moe_w3, moe_w2):
    depth = mod_w.shape[0]
    mod = _modulation(c, mod_w, mod_b)
    cos_t, sin_t = _rope_tables(positions)
    for l in range(depth):
        if l % 2 == 0:
            e = l // 2
            lambda_init = 0.8 - 0.6 * math.exp(-0.3 * l)
            qa, ka, va, qb, kb, vb = _ln_proj_even(
                x, mod[l], norm1_g[l], ev_w_in[e], ev_b_f[e], ev_qn_a[e], ev_kn_a[e], ev_qn_b[e],
                ev_kn_b[e], cos_t, sin_t)
            o = _even_mixer_attention(qa, ka, va, qb, kb, vb, ev_lam[e], ev_subln_g[e], lambda_init)
            w_out = ev_w_out[e]
        else:
            od = l // 2
            q, k, v = _ln_proj_odd(x, mod[l], norm1_g[l], od_w_in[od])
            o = _sb_attention(q, k, v)
            w_out = od_w_out[od]
        x1, h2, ids, wts = _out_proj_route(o, x, mod[l], norm2_g[l], w_out, moe_w_gr[l], moe_b_gr[l],
                                           moe_w_er[l], moe_b_er[l])
        x = _moe(h2, ids, wts, x1, mod[l], moe_w1, moe_w3, moe_w2, l)
    return x
```

```python
import functools
import math

import numpy as np
import jax
import jax.numpy as jnp
from jax import lax
from jax.experimental import pallas as pl
from jax.experimental.pallas import tpu as pltpu

F32 = jnp.float32
BF16 = jnp.bfloat16

D_MODEL = 1024
HEAD_DIM = 64
HALF = HEAD_DIM // 2
A_HEADS = 4
B_HEADS = 8
C_HEADS = 16
SEC = 512
N_GROUPS = 4
N_EXPERTS = 8
N_EXP_TOTAL = N_GROUPS * N_EXPERTS
ROPE_THETA = 10000.0
RMS_EPS = 1e-6
QK_SCALE = HEAD_DIM ** -0.5
LOG2E = math.log2(math.e)

LANES = 128
TM = 512
TQ_SOFTMAX = 4096
TQ_STICK = 1024
TK_SOFTMAX = 1024
TK_STICK = 512
SB_BLOCK = 256
DEAD_BITS = 160.0
FOX_AUG = 16
FOX_K = HEAD_DIM + FOX_AUG
T_EXP = 512
TC = 1024
RANK_TC = 1024
ISSUE_UNROLL = 8
NEG = -1e30
ROUTE_E0 = 8

NT_DIMS = (((1,), (1,)), ((), ()))
TN_DIMS = (((0,), (0,)), ((), ()))


def _dot(a, b):
    return jnp.dot(a, b, preferred_element_type=F32)


def _dot_nt(a, b):
    return lax.dot_general(a, b, NT_DIMS, preferred_element_type=F32)


def _dot_tn(a, b):
    return lax.dot_general(a, b, TN_DIMS, preferred_element_type=F32)


def _split2(x):
    hi = x.astype(BF16)
    lo = (x - hi.astype(F32)).astype(BF16)
    return hi, lo


def _split3(x):
    hi = x.astype(BF16)
    r = x - hi.astype(F32)
    mid = r.astype(BF16)
    lo = (r - mid.astype(F32)).astype(BF16)
    return hi, mid, lo


def _softplus_neg_abs(z):
    return jnp.log(1.0 + jnp.exp(-jnp.abs(z)))


def _log_sigmoid(z):
    return jnp.minimum(z, 0.0) - _softplus_neg_abs(z)


def _rms_rows(x, eps=RMS_EPS):
    return x * lax.rsqrt(jnp.mean(x * x, axis=-1, keepdims=True) + eps)


def _mod_kernel(c_ref, w_ref, b_ref, o_ref):
    c = c_ref[...]
    ca = c * jax.nn.sigmoid(c)
    c_hi, c_mid, c_lo = _split3(ca)
    w = w_ref[0]
    w_hi, w_lo = _split2(w)
    acc = _dot(c_hi, w_hi) + _dot(c_hi, w_lo) + _dot(c_mid, w_hi) + _dot(c_lo, w_hi) + _dot(c_mid, w_lo)
    o_ref[0] = acc + b_ref[0]


def _modulation(c, mod_w, mod_b):
    depth, d, n6 = mod_w.shape
    bsz = c.shape[0]
    rows = 8
    tn = 1536
    c_pad = jnp.zeros((rows, d), F32).at[:bsz].set(c)
    out = pl.pallas_call(
        _mod_kernel,
        grid=(depth, n6 // tn),
        in_specs=[
            pl.BlockSpec((rows, d), lambda l, j: (0, 0)),
            pl.BlockSpec((1, d, tn), lambda l, j: (l, 0, j)),
            pl.BlockSpec((1, 1, tn), lambda l, j: (l, 0, j)),
        ],
        out_specs=pl.BlockSpec((1, rows, tn), lambda l, j: (l, 0, j)),
        out_shape=jax.ShapeDtypeStruct((depth, rows, n6), F32),
        name="adaln_mod",
    )(c_pad, mod_w, mod_b.reshape(depth, 1, n6))
    return out[:, :bsz].reshape(depth, bsz, 6, d)


def _rope_table_kernel(pos_ref, inv_ref, cos_ref, sin_ref):
    ang = pos_ref[0].astype(F32) * inv_ref[...]
    cos_ref[0] = jnp.cos(ang)
    sin_ref[0] = jnp.sin(ang)


def _rope_tables(positions):
    bsz, seq = positions.shape
    ts = 2048
    inv = ROPE_THETA ** (-2.0 * jnp.arange(HALF, dtype=F32) / HEAD_DIM)
    return pl.pallas_call(
        _rope_table_kernel,
        grid=(bsz, seq // ts),
        in_specs=[
            pl.BlockSpec((1, 1, ts), lambda b, s: (b, 0, s)),
            pl.BlockSpec((HALF, 1), lambda b, s: (0, 0)),
        ],
        out_specs=[pl.BlockSpec((1, HALF, ts), lambda b, s: (b, 0, s))] * 2,
        out_shape=[jax.ShapeDtypeStruct((bsz, HALF, seq), F32)] * 2,
        name="rope_tables",
    )(positions.reshape(bsz, 1, seq), inv.reshape(HALF, 1))


def _adaln(x, mod_rows, g, first):
    shift = mod_rows[first:first + 1]
    scale = mod_rows[first + 1:first + 2]
    return _rms_rows(x) * g * (1.0 + scale) + shift


def _ln_proj_even_kernel(x_ref, mod_ref, g_ref, wT_ref, wfT_ref, bf_ref,
                         qna_ref, kna_ref, qnb_ref, knb_ref, cos_ref, sin_ref,
                         qa_o, ka_o, va_o, qb_o, kb_o, vb_o, carry):
    tm = x_ref.shape[1]

    @pl.when(pl.program_id(1) == 0)
    def _():
        carry[...] = jnp.zeros_like(carry)

    hb = _adaln(x_ref[0], mod_ref[0], g_ref[...], 0).astype(BF16)
    cos = cos_ref[0]
    sin = sin_ref[0]

    def section(idx):
        return _dot_nt(wT_ref[idx * SEC:(idx + 1) * SEC, :], hb)

    def norm_heads(p, g_col, out_ref, rope, scale, stride):
        for j in range(SEC // HEAD_DIM):
            xj = p[j * HEAD_DIM:(j + 1) * HEAD_DIM]
            yj = xj * lax.rsqrt(jnp.mean(xj * xj, axis=0, keepdims=True) + RMS_EPS) * g_col
            if rope:
                y1 = yj[:HALF]
                y2 = yj[HALF:]
                yj = jnp.concatenate([y1 * cos - y2 * sin, y2 * cos + y1 * sin], axis=0)
            out_ref[0, j * stride:j * stride + HEAD_DIM, :] = (yj * scale).astype(out_ref.dtype)

    softmax_q_scale = QK_SCALE * LOG2E
    norm_heads(section(0), qna_ref[...], qa_o, True, softmax_q_scale, HEAD_DIM)
    norm_heads(section(1), kna_ref[...], ka_o, True, 1.0, HEAD_DIM)
    va_o[0] = section(2).astype(va_o.dtype)
    norm_heads(section(3), qnb_ref[...], qb_o, False, softmax_q_scale, FOX_K)
    norm_heads(section(4), knb_ref[...], kb_o, False, 1.0, FOX_K)
    vb_o[0] = section(5).astype(vb_o.dtype)

    r = lax.broadcasted_iota(jnp.int32, (tm, tm), 0)
    c = lax.broadcasted_iota(jnp.int32, (tm, tm), 1)
    upto = jnp.where(r <= c, 1.0, 0.0).astype(BF16)
    log_f = _log_sigmoid(_dot_nt(wfT_ref[...], hb)[:B_HEADS] + bf_ref[...])
    f_hi, f_mid, f_lo = _split3(log_f)
    cum = _dot(f_hi, upto) + _dot(f_mid, upto) + _dot(f_lo, upto) + carry[...]
    carry[...] = cum[:, tm - 1:tm]

    row = lax.broadcasted_iota(jnp.int32, (FOX_AUG, tm), 0)
    for j in range(B_HEADS):
        pieces = [p.astype(F32) for p in _split3(cum[j:j + 1] * LOG2E)]
        q_aug = jnp.where(row < 3, 1.0, 0.0)
        k_aug = jnp.where((row >= 3) & (row < 6), 1.0, 0.0)
        for i, piece in enumerate(pieces):
            q_aug = jnp.where(row == 3 + i, piece, q_aug)
            k_aug = jnp.where(row == i, -piece, k_aug)
        lo = j * FOX_K + HEAD_DIM
        qb_o[0, lo:lo + FOX_AUG, :] = q_aug.astype(qb_o.dtype)
        kb_o[0, lo:lo + FOX_AUG, :] = k_aug.astype(kb_o.dtype)


def _ln_proj_even(x, mod, g, w_in, b_f, qn_a, kn_a, qn_b, kn_b, cos_t, sin_t):
    bsz, seq, d = x.shape
    n_main = 6 * SEC
    wT = w_in[:, :n_main].T.astype(BF16)
    wfT = jnp.zeros((16, d), F32).at[:B_HEADS].set(w_in[:, n_main:].T).astype(BF16)
    col = lambda v: v.reshape(HEAD_DIM, 1)
    full = lambda shape: pl.BlockSpec(shape, lambda b, s: (0,) * len(shape))
    rows_spec = lambda rows: pl.BlockSpec((1, rows, TM), lambda b, s: (b, 0, s))
    rows_shape = lambda rows: jax.ShapeDtypeStruct((bsz, rows, seq), BF16)
    out_rows = [SEC, SEC, SEC, B_HEADS * FOX_K, B_HEADS * FOX_K, SEC]
    return pl.pallas_call(
        _ln_proj_even_kernel,
        grid=(bsz, seq // TM),
        in_specs=[
            pl.BlockSpec((1, TM, d), lambda b, s: (b, s, 0)),
            pl.BlockSpec((1, 6, d), lambda b, s: (b, 0, 0)),
            full((1, d)), full((n_main, d)), full((16, d)), full((B_HEADS, 1)),
            full((HEAD_DIM, 1)), full((HEAD_DIM, 1)), full((HEAD_DIM, 1)), full((HEAD_DIM, 1)),
            rows_spec(HALF), rows_spec(HALF),
        ],
        out_specs=[rows_spec(n) for n in out_rows],
        out_shape=[rows_shape(n) for n in out_rows],
        scratch_shapes=[pltpu.VMEM((B_HEADS, 1), F32)],
        name="ln_proj_even",
    )(x, mod, g.reshape(1, d), wT, wfT, b_f.reshape(B_HEADS, 1), col(qn_a), col(kn_a), col(qn_b),
      col(kn_b), cos_t, sin_t)


def _ln_proj_odd_kernel(x_ref, mod_ref, g_ref, wT_ref, q_o, k_o, v_o):
    hb = _adaln(x_ref[0], mod_ref[0], g_ref[...], 0).astype(BF16)
    width = q_o.shape[1]
    for idx, (out_ref, scale) in enumerate(((q_o, QK_SCALE * LOG2E), (k_o, 1.0), (v_o, 1.0))):
        for half in range(2):
            lo = idx * width + half * (width // 2)
            p = _dot_nt(wT_ref[lo:lo + width // 2, :], hb)
            out_ref[0, half * (width // 2):(half + 1) * (width // 2), :] = (p * scale).astype(out_ref.dtype)


def _ln_proj_odd(x, mod, g, w_in):
    bsz, seq, d = x.shape
    width = C_HEADS * HEAD_DIM
    wT = w_in.T.astype(BF16)
    full = lambda shape: pl.BlockSpec(shape, lambda b, s: (0,) * len(shape))
    spec = pl.BlockSpec((1, width, TM), lambda b, s: (b, 0, s))
    shape = jax.ShapeDtypeStruct((bsz, width, seq), BF16)
    return pl.pallas_call(
        _ln_proj_odd_kernel,
        grid=(bsz, seq // TM),
        in_specs=[
            pl.BlockSpec((1, TM, d), lambda b, s: (b, s, 0)),
            pl.BlockSpec((1, 6, d), lambda b, s: (b, 0, 0)),
            full((1, d)), full((3 * width, d)),
        ],
        out_specs=[spec] * 3,
        out_shape=[shape] * 3,
        name="ln_proj_odd",
    )(x, mod, g.reshape(1, d), wT)


def _diag_valid(shape, strict):
    key = lax.broadcasted_iota(jnp.int32, shape, 0)
    qry = lax.broadcasted_iota(jnp.int32, shape, 1)
    return (qry > key) if strict else (qry >= key)


def _query_cols(diag):
    return slice(0 if diag is None else diag, None)


def _mask_triangle(x, strict, fill):
    tk = x.shape[0]
    square = jnp.where(_diag_valid((tk, tk), strict), x[:, :tk], fill)
    return square if x.shape[1] == tk else jnp.concatenate([square, x[:, tk:]], axis=1)


def _softmax_tile(kT, qT_ref, q_rows, vT, m_ref, l_ref, acc_ref, diag):
    qs = _query_cols(diag)
    sT = _dot_tn(kT, qT_ref[0, q_rows, qs])
    if diag is not None:
        sT = _mask_triangle(sT, False, NEG)
    m_prev = m_ref[:, qs]
    m_new = jnp.maximum(m_prev, jnp.max(sT, axis=0, keepdims=True))
    alpha = jnp.exp2(m_prev - m_new)
    p = jnp.exp2(sT - m_new)
    l_ref[:, qs] = alpha * l_ref[:, qs] + jnp.sum(p, axis=0, keepdims=True)
    acc_ref[:, qs] = alpha * acc_ref[:, qs] + _dot(vT, p.astype(BF16))
    m_ref[:, qs] = m_new


def _init_softmax(m_refs, l_refs, acc_refs):
    for r in m_refs:
        r[...] = jnp.full_like(r, -jnp.inf)
    for r in l_refs + acc_refs:
        r[...] = jnp.zeros_like(r)


def _key_tile(ki, tk):
    return pl.ds(pl.multiple_of(ki * tk, tk), tk)


def _attn_diff_kernel(qT_ref, kT_ref, vT_ref, lam_ref, sg_ref, o_ref,
                      m0, m1, l0, l1, acc0, acc1, *, lambda_init):
    qi = pl.program_id(2)
    ms, ls, accs = [m0, m1], [l0, l1], [acc0, acc1]
    _init_softmax(ms, ls, accs)

    def tiles(ki, diag):
        cols = _key_tile(ki, TK_SOFTMAX)
        for c in range(2):
            hs = slice(c * HEAD_DIM, (c + 1) * HEAD_DIM)
            _softmax_tile(kT_ref[0, hs, cols], qT_ref, hs, vT_ref[0, :, cols], ms[c], ls[c], accs[c], diag)

    def below_diagonal(ki, carry):
        tiles(ki, None)
        return carry

    diag_tiles = qT_ref.shape[2] // TK_SOFTMAX
    n_below = qi * diag_tiles
    lax.fori_loop(0, n_below, below_diagonal, 0)
    for d in range(diag_tiles):
        tiles(n_below + d, d * TK_SOFTMAX)
    lam = lam_ref[...]
    e1 = jnp.exp(jnp.sum(lam[0:1] * lam[1:2], axis=-1, keepdims=True))
    e2 = jnp.exp(jnp.sum(lam[2:3] * lam[3:4], axis=-1, keepdims=True))
    lam_full = e1 - e2 + lambda_init
    oT = acc0[...] * (1.0 / l0[...]) - lam_full * (acc1[...] * (1.0 / l1[...]))
    oT = oT * lax.rsqrt(jnp.mean(oT * oT, axis=0, keepdims=True) + RMS_EPS)
    oT = oT * sg_ref[...] * (1.0 - lambda_init)
    o_ref[0] = oT.T.astype(o_ref.dtype)


def _attn_fox_kernel(qT_ref, kT_ref, vT_ref, o_ref, m0, m1, l0, l1, acc0, acc1):
    qi = pl.program_id(2)
    ms, ls, accs = [m0, m1], [l0, l1], [acc0, acc1]
    _init_softmax(ms, ls, accs)

    def tiles(ki, diag):
        cols = _key_tile(ki, TK_SOFTMAX)
        for j in range(2):
            ks = slice(j * FOX_K, (j + 1) * FOX_K)
            vs = slice(j * HEAD_DIM, (j + 1) * HEAD_DIM)
            _softmax_tile(kT_ref[0, ks, cols], qT_ref, ks, vT_ref[0, vs, cols], ms[j], ls[j], accs[j], diag)

    def below_diagonal(ki, carry):
        tiles(ki, None)
        return carry

    diag_tiles = qT_ref.shape[2] // TK_SOFTMAX
    n_below = qi * diag_tiles
    lax.fori_loop(0, n_below, below_diagonal, 0)
    for d in range(diag_tiles):
        tiles(n_below + d, d * TK_SOFTMAX)
    oT = jnp.concatenate([acc0[...] * (1.0 / l0[...]), acc1[...] * (1.0 / l1[...])], axis=0)
    o_ref[0] = oT.T.astype(o_ref.dtype)


def _sb_tile(kT, qT_ref, q_rows, vT, later_ref, r_ref, acc_ref, diag):
    qs = _query_cols(diag)
    zT = _dot_tn(kT, qT_ref[0, q_rows, qs])
    neg_keep = jnp.maximum(zT, 0.0) + jnp.log2(1.0 + jnp.exp2(-jnp.abs(zT)))
    log_beta = zT - neg_keep
    if diag is not None:
        neg_keep = _mask_triangle(neg_keep, True, 0.0)
    keep_bf = neg_keep.astype(BF16)
    beyond = r_ref[:, qs]
    parts = []
    for blk in reversed(range(zT.shape[0] // SB_BLOCK)):
        rows = slice(blk * SB_BLOCK, (blk + 1) * SB_BLOCK)
        within = _dot(later_ref[...], keep_bf[rows])
        parts.append(within + beyond)
        beyond = beyond + within[0:1] + neg_keep[blk * SB_BLOCK:blk * SB_BLOCK + 1]
    laterT = jnp.concatenate(parts[::-1], axis=0)
    a = jnp.exp2(log_beta - laterT)
    if diag is not None:
        a = _mask_triangle(a, True, 0.0)
    acc_ref[:, qs] = acc_ref[:, qs] + _dot(vT, a.astype(BF16))
    r_ref[:, qs] = beyond


def _attn_sb_kernel(qT_ref, kT_ref, vT_ref, later_ref, o_ref, r0, r1, acc0, acc1):
    qi = pl.program_id(2)
    rs, accs = [r0, r1], [acc0, acc1]
    for r in rs + accs:
        r[...] = jnp.zeros_like(r)

    def tiles(ki, diag):
        cols = _key_tile(ki, TK_STICK)
        for j in range(2):
            hs = slice(j * HEAD_DIM, (j + 1) * HEAD_DIM)
            _sb_tile(kT_ref[0, hs, cols], qT_ref, hs, vT_ref[0, hs, cols], later_ref, rs[j], accs[j], diag)

    diag_tiles = qT_ref.shape[2] // TK_STICK
    n_below = qi * diag_tiles

    def least_decay():
        return jnp.minimum(jnp.min(r0[...]), jnp.min(r1[...]))

    def more_to_do(state):
        i, decay = state
        return (i < n_below) & (decay < DEAD_BITS)

    def below_diagonal(state):
        i, _ = state
        tiles(n_below - 1 - i, None)
        return i + 1, least_decay()

    for d in reversed(range(diag_tiles)):
        tiles(n_below + d, d * TK_STICK)
    lax.while_loop(more_to_do, below_diagonal, (0, least_decay()))
    o_ref[0] = jnp.concatenate([acc0[...], acc1[...]], axis=0).T.astype(o_ref.dtype)


def _attention_call(body, name, n_groups, tq, qT, kT, vT, qk_rows, v_rows, extra_in, extra_specs, n_stats,
                    v_dim):
    bsz, _, seq = qT.shape
    in_specs = [
        pl.BlockSpec((1, qk_rows, tq), lambda b, h, qi: (b, h, qi)),
        pl.BlockSpec((1, qk_rows, seq), lambda b, h, qi: (b, h, 0)),
        pl.BlockSpec((1, v_rows, seq), lambda b, h, qi: (b, h, 0)),
    ] + extra_specs
    scratch = [pltpu.VMEM((1, tq), F32)] * n_stats + [pltpu.VMEM((v_dim, tq), F32)] * 2
    return pl.pallas_call(
        body,
        grid=(bsz, n_groups, seq // tq),
        in_specs=in_specs,
        out_specs=pl.BlockSpec((1, tq, 2 * HEAD_DIM), lambda b, h, qi: (b, qi, h)),
        out_shape=jax.ShapeDtypeStruct((bsz, seq, n_groups * 2 * HEAD_DIM), BF16),
        scratch_shapes=scratch,
        name=name,
    )(qT, kT, vT, *extra_in)


def _even_mixer_attention(qa, ka, va, qb, kb, vb, lam, subln_g, lambda_init):
    full = lambda shape: pl.BlockSpec(shape, lambda b, h, qi: (0,) * len(shape))
    pair = 2 * HEAD_DIM
    o_a = _attention_call(
        functools.partial(_attn_diff_kernel, lambda_init=lambda_init), "attn_diff", A_HEADS, TQ_SOFTMAX,
        qa, ka, va, pair, pair, [lam, subln_g.reshape(pair, 1)],
        [full((4, HEAD_DIM)), full((pair, 1))], 4, pair)
    o_b = _attention_call(_attn_fox_kernel, "attn_fox", B_HEADS // 2, TQ_SOFTMAX, qb, kb, vb, 2 * FOX_K, pair,
                          [], [], 4, HEAD_DIM)
    return [o_a, o_b]


def _sb_attention(qT, kT, vT):
    full = lambda shape: pl.BlockSpec(shape, lambda b, h, qi: (0,) * len(shape))
    pair = 2 * HEAD_DIM
    later = jnp.triu(jnp.ones((SB_BLOCK, SB_BLOCK), F32), 1).astype(BF16)
    return [_attention_call(_attn_sb_kernel, "attn_sb", C_HEADS // 2, TQ_STICK, qT, kT, vT, pair, pair,
                            [later], [full((SB_BLOCK, SB_BLOCK))], 2, HEAD_DIM)]


def _out_proj_route_kernel(*refs, n_parts):
    o_refs = refs[:n_parts]
    wo_refs = refs[n_parts:2 * n_parts]
    x_ref, mod_ref, g_ref, wrh_ref, wrl_ref, br_ref, x1_o, h2_o, ids_o, wts_o = refs[2 * n_parts:]
    mod_rows = mod_ref[0]
    gate1 = mod_rows[2:3]
    mix = _dot(o_refs[0][0], wo_refs[0][...])
    for o_ref, wo_ref in zip(o_refs[1:], wo_refs[1:]):
        mix = mix + _dot(o_ref[0], wo_ref[...])
    x1 = x_ref[0] + gate1 * mix
    x1_o[0] = x1
    h2 = _adaln(x1, mod_rows, g_ref[...], 3)
    h2_o[0] = h2

    h_hi, h_lo = _split2(h2)
    wr_hi = wrh_ref[...]
    logits = _dot(h_hi, wr_hi) + _dot(h_hi, wrl_ref[...]) + _dot(h_lo, wr_hi) + br_ref[...]

    lane = lax.broadcasted_iota(jnp.int32, logits.shape, 1)
    big = jnp.int32(LANES)
    g_mask = lane < N_GROUPS
    g_log = jnp.where(g_mask, logits, NEG)
    g_max = jnp.max(g_log, axis=-1, keepdims=True)
    g_sum = jnp.sum(jnp.where(g_mask, jnp.exp(g_log - g_max), 0.0), axis=-1, keepdims=True)
    g_w = 1.0 / g_sum
    g_idx = jnp.min(jnp.where(g_mask & (g_log == g_max), lane, big), axis=-1, keepdims=True)

    e_mask = (lane >= ROUTE_E0) & (lane < ROUTE_E0 + N_EXP_TOTAL) & ((lane >> 3) == g_idx + 1)
    e_log = jnp.where(e_mask, logits, NEG)
    e_max = jnp.max(e_log, axis=-1, keepdims=True)
    e_exp = jnp.where(e_mask, jnp.exp(e_log - e_max), 0.0)
    e_prob = e_exp / jnp.sum(e_exp, axis=-1, keepdims=True)
    p1 = jnp.max(jnp.where(e_mask, e_prob, -1.0), axis=-1, keepdims=True)
    i1 = jnp.min(jnp.where(e_mask & (e_prob == p1), lane, big), axis=-1, keepdims=True)
    rest = e_mask & (lane != i1)
    p2 = jnp.max(jnp.where(rest, e_prob, -1.0), axis=-1, keepdims=True)
    i2 = jnp.min(jnp.where(rest & (e_prob == p2), lane, big), axis=-1, keepdims=True)
    den = p1 + p2
    w1 = p1 / den * g_w
    w2 = p2 / den * g_w
    ids_o[0] = jnp.where(lane == 0, i1 - ROUTE_E0, jnp.where(lane == 1, i2 - ROUTE_E0, 0))
    wts_o[0] = jnp.where(lane == 0, w1, jnp.where(lane == 1, w2, 0.0))


def _out_proj_route(o_parts, x, mod, g2, w_out, w_gr, b_gr, w_er, b_er):
    bsz, seq, d = x.shape
    widths = [o.shape[-1] for o in o_parts]
    starts = np.cumsum([0] + widths)
    wo_parts = [w_out[starts[i]:starts[i + 1]].astype(BF16) for i in range(len(o_parts))]
    wr = jnp.zeros((d, LANES), F32)
    wr = wr.at[:, :N_GROUPS].set(w_gr)
    wr = wr.at[:, ROUTE_E0:ROUTE_E0 + N_EXP_TOTAL].set(
        jnp.transpose(w_er, (1, 0, 2)).reshape(d, N_EXP_TOTAL))
    br = jnp.zeros((1, LANES), F32)
    br = br.at[0, :N_GROUPS].set(b_gr)
    br = br.at[0, ROUTE_E0:ROUTE_E0 + N_EXP_TOTAL].set(b_er.reshape(N_EXP_TOTAL))
    wr_hi = wr.astype(BF16)
    wr_lo = (wr - wr_hi.astype(F32)).astype(BF16)
    full = lambda shape: pl.BlockSpec(shape, lambda b, s: (0,) * len(shape))
    tile = lambda w: pl.BlockSpec((1, TM, w), lambda b, s: (b, s, 0))
    return pl.pallas_call(
        functools.partial(_out_proj_route_kernel, n_parts=len(o_parts)),
        grid=(bsz, seq // TM),
        in_specs=[tile(w) for w in widths] + [full((w, d)) for w in widths] + [
            tile(d),
            pl.BlockSpec((1, 6, d), lambda b, s: (b, 0, 0)),
            full((1, d)), full((d, LANES)), full((d, LANES)), full((1, LANES)),
        ],
        out_specs=[tile(d), tile(d), tile(LANES), tile(LANES)],
        out_shape=[
            jax.ShapeDtypeStruct((bsz, seq, d), F32),
            jax.ShapeDtypeStruct((bsz, seq, d), F32),
            jax.ShapeDtypeStruct((bsz, seq, LANES), jnp.int32),
            jax.ShapeDtypeStruct((bsz, seq, LANES), F32),
        ],
        name="out_proj_route",
    )(*o_parts, *wo_parts, x, mod, g2.reshape(1, d), wr_hi, wr_lo, br)


def _expert_onehots(ids_ref):
    tc = ids_ref.shape[2]
    expert = lax.broadcasted_iota(jnp.int32, (N_EXP_TOTAL, tc), 0)
    ids = ids_ref[0]
    return [jnp.where(expert == ids[k:k + 1], 1.0, 0.0) for k in range(2)]


def _rank_kernel(ids_ref, rank_o, cnt_o, carry):
    tc = ids_ref.shape[2]

    @pl.when(pl.program_id(0) == 0)
    def _():
        carry[...] = jnp.zeros_like(carry)

    r = lax.broadcasted_iota(jnp.int32, (tc, tc), 0)
    c = lax.broadcasted_iota(jnp.int32, (tc, tc), 1)
    before = jnp.where(r < c, 1.0, 0.0).astype(BF16)
    base = carry[...]
    rank_o[0] = jnp.zeros(rank_o.shape[1:], rank_o.dtype)
    for k, oh in enumerate(_expert_onehots(ids_ref)):
        prefix = _dot(oh.astype(BF16), before)
        rank_o[0, k:k + 1, :] = jnp.sum(oh * (base + prefix), axis=0, keepdims=True).astype(jnp.int32)
        base = base + jnp.sum(oh, axis=-1, keepdims=True)
    carry[...] = base
    cnt_o[...] = jnp.broadcast_to(base, cnt_o.shape)


def _pos_kernel(ids_ref, rank_ref, off_ref, pos_o):
    rank = rank_ref[0]
    pos_o[0] = jnp.zeros(pos_o.shape[1:], pos_o.dtype)
    for k, oh in enumerate(_expert_onehots(ids_ref)):
        off = jnp.sum(oh * off_ref[...], axis=0, keepdims=True).astype(jnp.int32)
        pos_o[0, k:k + 1, :] = off + rank[k:k + 1]


def _dispatch_plan(ids2):
    n = ids2.shape[0]
    nc = n // RANK_TC
    ids_t = jnp.zeros((nc, 8, RANK_TC), jnp.int32).at[:, :2].set(
        jnp.transpose(ids2.reshape(nc, RANK_TC, 2), (0, 2, 1)))
    blk = pl.BlockSpec((1, 8, RANK_TC), lambda c: (c, 0, 0))
    rank, cnt = pl.pallas_call(
        _rank_kernel,
        grid=(nc,),
        in_specs=[blk],
        out_specs=[blk, pl.BlockSpec((N_EXP_TOTAL, LANES), lambda c: (0, 0))],
        out_shape=[jax.ShapeDtypeStruct((nc, 8, RANK_TC), jnp.int32),
                   jax.ShapeDtypeStruct((N_EXP_TOTAL, LANES), F32)],
        scratch_shapes=[pltpu.VMEM((N_EXP_TOTAL, 1), F32)],
        name="moe_rank",
    )(ids_t)
    counts = cnt[:, 0].astype(jnp.int32)
    tiles_per = (counts + T_EXP - 1) // T_EXP
    tile_start = jnp.cumsum(tiles_per) - tiles_per
    n_used = jnp.sum(tiles_per)
    pos = pl.pallas_call(
        _pos_kernel,
        grid=(nc,),
        in_specs=[blk, blk, pl.BlockSpec((N_EXP_TOTAL, 1), lambda c: (0, 0))],
        out_specs=blk,
        out_shape=jax.ShapeDtypeStruct((nc, 8, RANK_TC), jnp.int32),
        name="moe_pos",
    )(ids_t, rank, (tile_start * T_EXP).astype(F32).reshape(N_EXP_TOTAL, 1))
    n_tiles = (2 * n) // T_EXP + N_EXP_TOTAL
    tile_idx = jnp.minimum(jnp.arange(n_tiles, dtype=jnp.int32), n_used - 1)
    tile_expert = jnp.sum((tile_start[None, :] <= tile_idx[:, None]).astype(jnp.int32), axis=1) - 1
    per = RANK_TC // TC
    pos_chunks = jnp.transpose(pos[:, :2].reshape(nc, 2, per, TC), (0, 2, 1, 3)).reshape(n // TC, 1, 2 * TC)
    last_tile = jnp.where(tiles_per > 0, tile_start + tiles_per - 1, -1)
    return pos_chunks, tile_expert, n_used.reshape(1).astype(jnp.int32), n_tiles, last_tile


def _row_copy_wait(src, dst, sem, rows):
    pltpu.make_async_copy(src.at[pl.ds(0, rows)], dst.at[pl.ds(0, rows)], sem).wait()


def _dispatch_kernel(last_ref, nu_ref, pos_ref, h_ref, xs_hbm, zero_s, sem):
    @pl.when(pl.program_id(0) == 0)
    def _():
        zero_s[...] = jnp.zeros_like(zero_s)

        def zero_tile(j):
            return pltpu.make_async_copy(zero_s, xs_hbm.at[pl.ds(pl.multiple_of(j * T_EXP, T_EXP), T_EXP)], sem)

        for e in range(N_EXP_TOTAL):
            @pl.when(last_ref[e] >= 0)
            def _():
                zero_tile(last_ref[e]).start()
        for e in range(N_EXP_TOTAL):
            @pl.when(last_ref[e] >= 0)
            def _():
                zero_tile(0).wait()

        def unused_tile(j, carry):
            tile = zero_tile(j)
            tile.start()
            tile.wait()
            return carry

        lax.fori_loop(nu_ref[0], xs_hbm.shape[0] // T_EXP, unused_tile, 0)

    def issue(i, carry):
        for k in range(2):
            dst = pos_ref[0, 0, k * TC + i]
            pltpu.make_async_copy(h_ref.at[pl.ds(i, 1)], xs_hbm.at[pl.ds(dst, 1)], sem).start()
        return carry

    lax.fori_loop(0, TC, issue, 0, unroll=ISSUE_UNROLL)
    for k in range(2):
        _row_copy_wait(h_ref, xs_hbm, sem, TC)


def _dispatch(h2_flat, pos_chunks, last_tile, n_used, n_rows):
    n, d = h2_flat.shape
    grid_spec = pltpu.PrefetchScalarGridSpec(
        num_scalar_prefetch=2,
        grid=(n // TC,),
        in_specs=[
            pl.BlockSpec((1, 1, 2 * TC), lambda c, pad, nu: (c, 0, 0), memory_space=pltpu.SMEM),
            pl.BlockSpec((TC, d), lambda c, pad, nu: (c, 0)),
        ],
        out_specs=pl.BlockSpec(memory_space=pl.ANY),
        scratch_shapes=[pltpu.VMEM((T_EXP, d), h2_flat.dtype), pltpu.SemaphoreType.DMA(())],
    )
    return pl.pallas_call(
        _dispatch_kernel,
        grid_spec=grid_spec,
        out_shape=jax.ShapeDtypeStruct((n_rows, d), h2_flat.dtype),
        name="moe_dispatch",
    )(last_tile, n_used, pos_chunks, h2_flat)


def _expert_kernel(te_ref, nu_ref, xs_ref, w1_ref, w3_ref, w2_ref, ys_ref, w1_s, w3_s, w2_s):
    j = pl.program_id(0)
    prev = te_ref[jnp.maximum(j - 1, 0)]

    @pl.when((j == 0) | (te_ref[j] != prev))
    def _():
        w1_s[...] = w1_ref[0].astype(BF16)
        w3_s[...] = w3_ref[0].astype(BF16)
        w2_s[...] = w2_ref[0].astype(BF16)

    @pl.when(j < nu_ref[0])
    def _():
        xb = xs_ref[...].astype(BF16)
        a = _dot(xb, w1_s[...])
        b = _dot(xb, w3_s[...])
        hid = a * jax.nn.sigmoid(a) * b
        ys_ref[...] = _dot(hid.astype(BF16), w2_s[...])

    @pl.when(j >= nu_ref[0])
    def _():
        ys_ref[...] = jnp.zeros_like(ys_ref)


def _experts(xs, n_tiles, tile_expert, n_used, w1, w3, w2, layer):
    d = xs.shape[1]
    n_rows = n_tiles * T_EXP
    f = w1.shape[-1]
    n_all = w1.shape[0] * N_EXP_TOTAL
    tile_expert = tile_expert + layer * N_EXP_TOTAL
    grid_spec = pltpu.PrefetchScalarGridSpec(
        num_scalar_prefetch=2,
        grid=(n_tiles,),
        in_specs=[
            pl.BlockSpec((T_EXP, d), lambda j, te, nu: (jnp.minimum(j, nu[0] - 1), 0)),
            pl.BlockSpec((1, d, f), lambda j, te, nu: (te[j], 0, 0)),
            pl.BlockSpec((1, d, f), lambda j, te, nu: (te[j], 0, 0)),
            pl.BlockSpec((1, f, d), lambda j, te, nu: (te[j], 0, 0)),
        ],
        out_specs=pl.BlockSpec((T_EXP, d), lambda j, te, nu: (j, 0)),
        scratch_shapes=[pltpu.VMEM((d, f), BF16), pltpu.VMEM((d, f), BF16), pltpu.VMEM((f, d), BF16)],
    )
    return pl.pallas_call(
        _expert_kernel,
        grid_spec=grid_spec,
        out_shape=jax.ShapeDtypeStruct((n_rows, d), F32),
        name="moe_experts",
    )(tile_expert, n_used, xs, w1.reshape(n_all, d, f), w3.reshape(n_all, d, f), w2.reshape(n_all, f, d))


def _combine_kernel(pos_ref, ys_hbm, x1_ref, wts_ref, mod_ref, x2_o, buf, sem):
    def issue(i, carry):
        for k in range(2):
            src = pos_ref[0, 0, k * TC + i]
            pltpu.make_async_copy(ys_hbm.at[pl.ds(src, 1)], buf.at[k, pl.ds(i, 1)], sem).start()
        return carry

    lax.fori_loop(0, TC, issue, 0, unroll=ISSUE_UNROLL)
    for k in range(2):
        _row_copy_wait(ys_hbm, buf.at[k], sem, TC)
    wts = wts_ref[0]
    y = wts[:, 0:1] * buf[0] + wts[:, 1:2] * buf[1]
    gate2 = mod_ref[0][5:6]
    x2_o[0] = x1_ref[0] + gate2 * y


def _combine(ys, pos_chunks, x1, wts, mod):
    bsz, seq, d = x1.shape
    per_b = seq // TC
    return pl.pallas_call(
        _combine_kernel,
        grid=(bsz, per_b),
        in_specs=[
            pl.BlockSpec((1, 1, 2 * TC), lambda b, s: (b * per_b + s, 0, 0), memory_space=pltpu.SMEM),
            pl.BlockSpec(memory_space=pl.ANY),
            pl.BlockSpec((1, TC, d), lambda b, s: (b, s, 0)),
            pl.BlockSpec((1, TC, LANES), lambda b, s: (b, s, 0)),
            pl.BlockSpec((1, 6, d), lambda b, s: (b, 0, 0)),
        ],
        out_specs=pl.BlockSpec((1, TC, d), lambda b, s: (b, s, 0)),
        out_shape=jax.ShapeDtypeStruct((bsz, seq, d), F32),
        scratch_shapes=[pltpu.VMEM((2, TC, d), F32), pltpu.SemaphoreType.DMA(())],
        name="moe_combine",
    )(pos_chunks, ys, x1, wts, mod)


def _moe(h2, ids, wts, x1, mod, w1, w3, w2, layer):
    bsz, seq, d = h2.shape
    n = bsz * seq
    pos_chunks, tile_expert, n_used, n_tiles, last_tile = _dispatch_plan(ids.reshape(n, LANES)[:, :2])
    xs = _dispatch(h2.reshape(n, d), pos_chunks, last_tile, n_used, n_tiles * T_EXP)
    ys = _experts(xs, n_tiles, tile_expert, n_used, w1, w3, w2, layer)
    return _combine(ys, pos_chunks, x1, wts, mod)


def kernel(x, c, positions, mod_w, mod_b, norm1_g, norm2_g, ev_w_in, ev_b_f, ev_qn_a, ev_kn_a, ev_lam,
           ev_subln_g, ev_qn_b, ev_kn_b, ev_w_out, od_w_in, od_w_out, moe_w_gr, moe_b_gr, moe_w_er,
           moe_b_er, moe_w1, moe_w3, moe_w2):
    depth = mod_w.shape[0]
    mod = _modulation(c, mod_w, mod_b)
    cos_t, sin_t = _rope_tables(positions)
    for l in range(depth):
        if l % 2 == 0:
            e = l // 2
            lambda_init = 0.8 - 0.6 * math.exp(-0.3 * l)
            qa, ka, va, qb, kb, vb = _ln_proj_even(
                x, mod[l], norm1_g[l], ev_w_in[e], ev_b_f[e], ev_qn_a[e], ev_kn_a[e], ev_qn_b[e],
                ev_kn_b[e], cos_t, sin_t)
            o = _even_mixer_attention(qa, ka, va, qb, kb, vb, ev_lam[e], ev_subln_g[e], lambda_init)
            w_out = ev_w_out[e]
        else:
            od = l // 2
            q, k, v = _ln_proj_odd(x, mod[l], norm1_g[l], od_w_in[od])
            o = _sb_attention(q, k, v)
            w_out = od_w_out[od]
        x1, h2, ids, wts = _out_proj_route(o, x, mod[l], norm2_g[l], w_out, moe_w_gr[l], moe_b_gr[l],
                                           moe_w_er[l], moe_b_er[l])
        x = _moe(h2, ids, wts, x1, mod[l], moe_w1, moe_w3, moe_w2, l)
    return x
```

```python
import functools
import math

import numpy as np
import jax
import jax.numpy as jnp
from jax import lax
from jax.experimental import pallas as pl
from jax.experimental.pallas import tpu as pltpu

F32 = jnp.float32
BF16 = jnp.bfloat16

D_MODEL = 1024
HEAD_DIM = 64
HALF = HEAD_DIM // 2
A_HEADS = 4
B_HEADS = 8
C_HEADS = 16
SEC = 512
N_GROUPS = 4
N_EXPERTS = 8
N_EXP_TOTAL = N_GROUPS * N_EXPERTS
ROPE_THETA = 10000.0
RMS_EPS = 1e-6
QK_SCALE = HEAD_DIM ** -0.5
LOG2E = math.log2(math.e)

LANES = 128
TM = 512
TQ_SOFTMAX = 4096
TQ_STICK = 1024
TK_SOFTMAX = 1024
TK_STICK = 512
SOFTMAX_QC = 4096
SB_BLOCK = 256
DEAD_BITS = 160.0
FOX_AUG = 16
FOX_K = HEAD_DIM + FOX_AUG
T_EXP = 512
TC = 1024
RANK_TC = 1024
ISSUE_UNROLL = 32
NEG = -1e30
ROUTE_E0 = 8

NT_DIMS = (((1,), (1,)), ((), ()))
TN_DIMS = (((0,), (0,)), ((), ()))


def _dot(a, b):
    return jnp.dot(a, b, preferred_element_type=F32)


def _dot_nt(a, b):
    return lax.dot_general(a, b, NT_DIMS, preferred_element_type=F32)


def _dot_tn(a, b):
    return lax.dot_general(a, b, TN_DIMS, preferred_element_type=F32)


def _split2(x):
    hi = x.astype(BF16)
    lo = (x - hi.astype(F32)).astype(BF16)
    return hi, lo


def _split3(x):
    hi = x.astype(BF16)
    r = x - hi.astype(F32)
    mid = r.astype(BF16)
    lo = (r - mid.astype(F32)).astype(BF16)
    return hi, mid, lo


def _softplus_neg_abs(z):
    return jnp.log(1.0 + jnp.exp(-jnp.abs(z)))


def _log_sigmoid(z):
    return jnp.minimum(z, 0.0) - _softplus_neg_abs(z)


def _rms_rows(x, eps=RMS_EPS):
    return x * lax.rsqrt(jnp.mean(x * x, axis=-1, keepdims=True) + eps)


def _mod_kernel(c_ref, w_ref, b_ref, o_ref):
    c = c_ref[...]
    ca = c * jax.nn.sigmoid(c)
    c_hi, c_mid, c_lo = _split3(ca)
    w = w_ref[0]
    w_hi, w_lo = _split2(w)
    acc = _dot(c_hi, w_hi) + _dot(c_hi, w_lo) + _dot(c_mid, w_hi) + _dot(c_lo, w_hi) + _dot(c_mid, w_lo)
    o_ref[0] = acc + b_ref[0]


def _modulation(c, mod_w, mod_b):
    depth, d, n6 = mod_w.shape
    bsz = c.shape[0]
    rows = 8
    tn = 1536
    c_pad = jnp.zeros((rows, d), F32).at[:bsz].set(c)
    out = pl.pallas_call(
        _mod_kernel,
        grid=(depth, n6 // tn),
        in_specs=[
            pl.BlockSpec((rows, d), lambda l, j: (0, 0)),
            pl.BlockSpec((1, d, tn), lambda l, j: (l, 0, j)),
            pl.BlockSpec((1, 1, tn), lambda l, j: (l, 0, j)),
        ],
        out_specs=pl.BlockSpec((1, rows, tn), lambda l, j: (l, 0, j)),
        out_shape=jax.ShapeDtypeStruct((depth, rows, n6), F32),
        name="adaln_mod",
    )(c_pad, mod_w, mod_b.reshape(depth, 1, n6))
    return out[:, :bsz].reshape(depth, bsz, 6, d)


def _rope_table_kernel(pos_ref, inv_ref, cos_ref, sin_ref):
    ang = pos_ref[0].astype(F32) * inv_ref[...]
    cos_ref[0] = jnp.cos(ang)
    sin_ref[0] = jnp.sin(ang)


def _rope_tables(positions):
    bsz, seq = positions.shape
    ts = 2048
    inv = ROPE_THETA ** (-2.0 * jnp.arange(HALF, dtype=F32) / HEAD_DIM)
    return pl.pallas_call(
        _rope_table_kernel,
        grid=(bsz, seq // ts),
        in_specs=[
            pl.BlockSpec((1, 1, ts), lambda b, s: (b, 0, s)),
            pl.BlockSpec((HALF, 1), lambda b, s: (0, 0)),
        ],
        out_specs=[pl.BlockSpec((1, HALF, ts), lambda b, s: (b, 0, s))] * 2,
        out_shape=[jax.ShapeDtypeStruct((bsz, HALF, seq), F32)] * 2,
        name="rope_tables",
    )(positions.reshape(bsz, 1, seq), inv.reshape(HALF, 1))


def _adaln(x, mod_rows, g, first):
    shift = mod_rows[first:first + 1]
    scale = mod_rows[first + 1:first + 2]
    return _rms_rows(x) * g * (1.0 + scale) + shift


def _ln_proj_even_kernel(x_ref, mod_ref, g_ref, wT_ref, wfT_ref, bf_ref,
                         qna_ref, kna_ref, qnb_ref, knb_ref, cos_ref, sin_ref,
                         qa_o, ka_o, va_o, qb_o, kb_o, vb_o, carry):
    tm = x_ref.shape[1]

    @pl.when(pl.program_id(1) == 0)
    def _():
        carry[...] = jnp.zeros_like(carry)

    hb = _adaln(x_ref[0], mod_ref[0], g_ref[...], 0).astype(BF16)
    cos = cos_ref[0]
    sin = sin_ref[0]

    def section(idx):
        return _dot_nt(wT_ref[idx * SEC:(idx + 1) * SEC, :], hb)

    def norm_heads(p, g_col, out_ref, rope, scale, stride):
        for j in range(SEC // HEAD_DIM):
            xj = p[j * HEAD_DIM:(j + 1) * HEAD_DIM]
            yj = xj * lax.rsqrt(jnp.mean(xj * xj, axis=0, keepdims=True) + RMS_EPS) * g_col
            if rope:
                y1 = yj[:HALF]
                y2 = yj[HALF:]
                yj = jnp.concatenate([y1 * cos - y2 * sin, y2 * cos + y1 * sin], axis=0)
            out_ref[0, j * stride:j * stride + HEAD_DIM, :] = (yj * scale).astype(out_ref.dtype)

    softmax_q_scale = QK_SCALE * LOG2E
    norm_heads(section(0), qna_ref[...], qa_o, True, softmax_q_scale, HEAD_DIM)
    norm_heads(section(1), kna_ref[...], ka_o, True, 1.0, HEAD_DIM)
    va_o[0] = section(2).astype(va_o.dtype)
    norm_heads(section(3), qnb_ref[...], qb_o, False, softmax_q_scale, FOX_K)
    norm_heads(section(4), knb_ref[...], kb_o, False, 1.0, FOX_K)
    vb_o[0] = section(5).astype(vb_o.dtype)

    r = lax.broadcasted_iota(jnp.int32, (tm, tm), 0)
    c = lax.broadcasted_iota(jnp.int32, (tm, tm), 1)
    upto = jnp.where(r <= c, 1.0, 0.0).astype(BF16)
    log_f = _log_sigmoid(_dot_nt(wfT_ref[...], hb)[:B_HEADS] + bf_ref[...])
    f_hi, f_mid, f_lo = _split3(log_f)
    cum = _dot(f_hi, upto) + _dot(f_mid, upto) + _dot(f_lo, upto) + carry[...]
    carry[...] = cum[:, tm - 1:tm]

    row = lax.broadcasted_iota(jnp.int32, (FOX_AUG, tm), 0)
    for j in range(B_HEADS):
        pieces = [p.astype(F32) for p in _split3(cum[j:j + 1] * LOG2E)]
        q_aug = jnp.where(row < 3, 1.0, 0.0)
        k_aug = jnp.where((row >= 3) & (row < 6), 1.0, 0.0)
        for i, piece in enumerate(pieces):
            q_aug = jnp.where(row == 3 + i, piece, q_aug)
            k_aug = jnp.where(row == i, -piece, k_aug)
        lo = j * FOX_K + HEAD_DIM
        qb_o[0, lo:lo + FOX_AUG, :] = q_aug.astype(qb_o.dtype)
        kb_o[0, lo:lo + FOX_AUG, :] = k_aug.astype(kb_o.dtype)


def _ln_proj_even(x, mod, g, w_in, b_f, qn_a, kn_a, qn_b, kn_b, cos_t, sin_t):
    bsz, seq, d = x.shape
    n_main = 6 * SEC
    wT = w_in[:, :n_main].T.astype(BF16)
    wfT = jnp.zeros((16, d), F32).at[:B_HEADS].set(w_in[:, n_main:].T).astype(BF16)
    col = lambda v: v.reshape(HEAD_DIM, 1)
    full = lambda shape: pl.BlockSpec(shape, lambda b, s: (0,) * len(shape))
    rows_spec = lambda rows: pl.BlockSpec((1, rows, TM), lambda b, s: (b, 0, s))
    rows_shape = lambda rows: jax.ShapeDtypeStruct((bsz, rows, seq), BF16)
    out_rows = [SEC, SEC, SEC, B_HEADS * FOX_K, B_HEADS * FOX_K, SEC]
    return pl.pallas_call(
        _ln_proj_even_kernel,
        grid=(bsz, seq // TM),
        in_specs=[
            pl.BlockSpec((1, TM, d), lambda b, s: (b, s, 0)),
            pl.BlockSpec((1, 6, d), lambda b, s: (b, 0, 0)),
            full((1, d)), full((n_main, d)), full((16, d)), full((B_HEADS, 1)),
            full((HEAD_DIM, 1)), full((HEAD_DIM, 1)), full((HEAD_DIM, 1)), full((HEAD_DIM, 1)),
            rows_spec(HALF), rows_spec(HALF),
        ],
        out_specs=[rows_spec(n) for n in out_rows],
        out_shape=[rows_shape(n) for n in out_rows],
        scratch_shapes=[pltpu.VMEM((B_HEADS, 1), F32)],
        name="ln_proj_even",
    )(x, mod, g.reshape(1, d), wT, wfT, b_f.reshape(B_HEADS, 1), col(qn_a), col(kn_a), col(qn_b),
      col(kn_b), cos_t, sin_t)


def _ln_proj_odd_kernel(x_ref, mod_ref, g_ref, wT_ref, q_o, k_o, v_o):
    hb = _adaln(x_ref[0], mod_ref[0], g_ref[...], 0).astype(BF16)
    width = q_o.shape[1]
    for idx, (out_ref, scale) in enumerate(((q_o, QK_SCALE * LOG2E), (k_o, 1.0), (v_o, 1.0))):
        for half in range(2):
            lo = idx * width + half * (width // 2)
            p = _dot_nt(wT_ref[lo:lo + width // 2, :], hb)
            out_ref[0, half * (width // 2):(half + 1) * (width // 2), :] = (p * scale).astype(out_ref.dtype)


def _ln_proj_odd(x, mod, g, w_in):
    bsz, seq, d = x.shape
    width = C_HEADS * HEAD_DIM
    wT = w_in.T.astype(BF16)
    full = lambda shape: pl.BlockSpec(shape, lambda b, s: (0,) * len(shape))
    spec = pl.BlockSpec((1, width, TM), lambda b, s: (b, 0, s))
    shape = jax.ShapeDtypeStruct((bsz, width, seq), BF16)
    return pl.pallas_call(
        _ln_proj_odd_kernel,
        grid=(bsz, seq // TM),
        in_specs=[
            pl.BlockSpec((1, TM, d), lambda b, s: (b, s, 0)),
            pl.BlockSpec((1, 6, d), lambda b, s: (b, 0, 0)),
            full((1, d)), full((3 * width, d)),
        ],
        out_specs=[spec] * 3,
        out_shape=[shape] * 3,
        name="ln_proj_odd",
    )(x, mod, g.reshape(1, d), wT)


def _diag_valid(shape, strict):
    key = lax.broadcasted_iota(jnp.int32, shape, 0)
    qry = lax.broadcasted_iota(jnp.int32, shape, 1)
    return (qry > key) if strict else (qry >= key)


def _query_cols(diag):
    return slice(0 if diag is None else diag, None)


def _mask_triangle(x, strict, fill):
    tk = x.shape[0]
    square = jnp.where(_diag_valid((tk, tk), strict), x[:, :tk], fill)
    return square if x.shape[1] == tk else jnp.concatenate([square, x[:, tk:]], axis=1)


def _softmax_tile(kT, qT_ref, q_rows, vT, m_ref, l_ref, acc_ref, diag):
    tq = qT_ref.shape[2]
    start = 0 if diag is None else diag
    for c0 in range(start, tq, SOFTMAX_QC):
        qs = slice(c0, min(c0 + SOFTMAX_QC, tq))
        sT = _dot_tn(kT, qT_ref[0, q_rows, qs])
        if diag is not None and c0 == start:
            sT = _mask_triangle(sT, False, NEG)
        m_prev = m_ref[:, qs]
        m_new = jnp.maximum(m_prev, jnp.max(sT, axis=0, keepdims=True))
        alpha = jnp.exp2(m_prev - m_new)
        p = jnp.exp2(sT - m_new)
        l_ref[:, qs] = alpha * l_ref[:, qs] + jnp.sum(p, axis=0, keepdims=True)
        acc_ref[:, qs] = alpha * acc_ref[:, qs] + _dot(vT, p.astype(BF16))
        m_ref[:, qs] = m_new


def _init_softmax(m_refs, l_refs, acc_refs):
    for r in m_refs:
        r[...] = jnp.full_like(r, -jnp.inf)
    for r in l_refs + acc_refs:
        r[...] = jnp.zeros_like(r)


def _key_tile(ki, tk):
    return pl.ds(pl.multiple_of(ki * tk, tk), tk)


def _attn_diff_kernel(qT_ref, kT_ref, vT_ref, lam_ref, sg_ref, o_ref,
                      m0, m1, l0, l1, acc0, acc1, *, lambda_init):
    qi = pl.program_id(2)
    ms, ls, accs = [m0, m1], [l0, l1], [acc0, acc1]
    _init_softmax(ms, ls, accs)

    def tiles(ki, diag):
        cols = _key_tile(ki, TK_SOFTMAX)
        for c in range(2):
            hs = slice(c * HEAD_DIM, (c + 1) * HEAD_DIM)
            _softmax_tile(kT_ref[0, hs, cols], qT_ref, hs, vT_ref[0, :, cols], ms[c], ls[c], accs[c], diag)

    def below_diagonal(ki, carry):
        tiles(ki, None)
        return carry

    diag_tiles = qT_ref.shape[2] // TK_SOFTMAX
    n_below = qi * diag_tiles
    lax.fori_loop(0, n_below, below_diagonal, 0)
    for d in range(diag_tiles):
        tiles(n_below + d, d * TK_SOFTMAX)
    lam = lam_ref[...]
    e1 = jnp.exp(jnp.sum(lam[0:1] * lam[1:2], axis=-1, keepdims=True))
    e2 = jnp.exp(jnp.sum(lam[2:3] * lam[3:4], axis=-1, keepdims=True))
    lam_full = e1 - e2 + lambda_init
    oT = acc0[...] * (1.0 / l0[...]) - lam_full * (acc1[...] * (1.0 / l1[...]))
    oT = oT * lax.rsqrt(jnp.mean(oT * oT, axis=0, keepdims=True) + RMS_EPS)
    oT = oT * sg_ref[...] * (1.0 - lambda_init)
    o_ref[0] = oT.T.astype(o_ref.dtype)


def _attn_fox_kernel(qT_ref, kT_ref, vT_ref, o_ref, m0, m1, l0, l1, acc0, acc1):
    qi = pl.program_id(2)
    ms, ls, accs = [m0, m1], [l0, l1], [acc0, acc1]
    _init_softmax(ms, ls, accs)

    def tiles(ki, diag):
        cols = _key_tile(ki, TK_SOFTMAX)
        for j in range(2):
            ks = slice(j * FOX_K, (j + 1) * FOX_K)
            vs = slice(j * HEAD_DIM, (j + 1) * HEAD_DIM)
            _softmax_tile(kT_ref[0, ks, cols], qT_ref, ks, vT_ref[0, vs, cols], ms[j], ls[j], accs[j], diag)

    def below_diagonal(ki, carry):
        tiles(ki, None)
        return carry

    diag_tiles = qT_ref.shape[2] // TK_SOFTMAX
    n_below = qi * diag_tiles
    lax.fori_loop(0, n_below, below_diagonal, 0)
    for d in range(diag_tiles):
        tiles(n_below + d, d * TK_SOFTMAX)
    oT = jnp.concatenate([acc0[...] * (1.0 / l0[...]), acc1[...] * (1.0 / l1[...])], axis=0)
    o_ref[0] = oT.T.astype(o_ref.dtype)


def _sb_tile(kT, qT_ref, q_rows, vT, later_ref, r_ref, acc_ref, diag, q_stop=None):
    qs = slice(0 if diag is None else diag, q_stop)
    zT = _dot_tn(kT, qT_ref[0, q_rows, qs])
    neg_keep = jnp.maximum(zT, 0.0) + jnp.log2(1.0 + jnp.exp2(-jnp.abs(zT)))
    log_beta = zT - neg_keep
    if diag is not None:
        neg_keep = _mask_triangle(neg_keep, True, 0.0)
    keep_bf = neg_keep.astype(BF16)
    beyond = r_ref[:, qs]
    parts = []
    for blk in reversed(range(zT.shape[0] // SB_BLOCK)):
        rows = slice(blk * SB_BLOCK, (blk + 1) * SB_BLOCK)
        within = _dot(later_ref[...], keep_bf[rows])
        parts.append(within + beyond)
        beyond = beyond + within[0:1] + neg_keep[blk * SB_BLOCK:blk * SB_BLOCK + 1]
    laterT = jnp.concatenate(parts[::-1], axis=0)
    a = jnp.exp2(log_beta - laterT)
    if diag is not None:
        a = _mask_triangle(a, True, 0.0)
    acc_ref[:, qs] = acc_ref[:, qs] + _dot(vT, a.astype(BF16))
    r_ref[:, qs] = beyond


def _attn_sb_kernel(qT_ref, kT_ref, vT_ref, later_ref, o_ref, r0, r1, acc0, acc1):
    qi = pl.program_id(2)
    rs, accs = [r0, r1], [acc0, acc1]
    for r in rs + accs:
        r[...] = jnp.zeros_like(r)

    def tiles(ki, diag, q_stop=None):
        cols = _key_tile(ki, TK_STICK)
        for j in range(2):
            hs = slice(j * HEAD_DIM, (j + 1) * HEAD_DIM)
            _sb_tile(kT_ref[0, hs, cols], qT_ref, hs, vT_ref[0, hs, cols], later_ref, rs[j], accs[j], diag,
                     q_stop)

    tq = qT_ref.shape[2]
    diag_tiles = tq // TK_STICK
    n_below = qi * diag_tiles
    head = min(TK_STICK, tq)

    def least_decay(lo, hi):
        return jnp.minimum(jnp.min(r0[:, lo:hi]), jnp.min(r1[:, lo:hi]))

    def more_to_do(state):
        i, head_decay, rest_decay = state
        return (i < n_below) & (jnp.minimum(head_decay, rest_decay) < DEAD_BITS)

    def below_diagonal(state):
        i, _, rest_decay = state
        ki = n_below - 1 - i
        if head < tq:
            @pl.when(rest_decay >= DEAD_BITS)
            def _():
                tiles(ki, None, head)

            @pl.when(rest_decay < DEAD_BITS)
            def _():
                tiles(ki, None)
        else:
            tiles(ki, None)
        return i + 1, least_decay(0, head), (least_decay(head, tq) if head < tq else jnp.float32(DEAD_BITS))

    for d in reversed(range(diag_tiles)):
        tiles(n_below + d, d * TK_STICK)
    start = (0, least_decay(0, head), least_decay(head, tq) if head < tq else jnp.float32(DEAD_BITS))
    lax.while_loop(more_to_do, below_diagonal, start)
    o_ref[0] = jnp.concatenate([acc0[...], acc1[...]], axis=0).T.astype(o_ref.dtype)


def _attention_call(body, name, n_groups, tq, qT, kT, vT, qk_rows, v_rows, extra_in, extra_specs, n_stats,
                    v_dim):
    bsz, _, seq = qT.shape
    in_specs = [
        pl.BlockSpec((1, qk_rows, tq), lambda b, h, qi: (b, h, qi)),
        pl.BlockSpec((1, qk_rows, seq), lambda b, h, qi: (b, h, 0)),
        pl.BlockSpec((1, v_rows, seq), lambda b, h, qi: (b, h, 0)),
    ] + extra_specs
    scratch = [pltpu.VMEM((1, tq), F32)] * n_stats + [pltpu.VMEM((v_dim, tq), F32)] * 2
    return pl.pallas_call(
        body,
        grid=(bsz, n_groups, seq // tq),
        in_specs=in_specs,
        out_specs=pl.BlockSpec((1, tq, 2 * HEAD_DIM), lambda b, h, qi: (b, qi, h)),
        out_shape=jax.ShapeDtypeStruct((bsz, seq, n_groups * 2 * HEAD_DIM), BF16),
        scratch_shapes=scratch,
        name=name,
    )(qT, kT, vT, *extra_in)


def _even_mixer_attention(qa, ka, va, qb, kb, vb, lam, subln_g, lambda_init):
    full = lambda shape: pl.BlockSpec(shape, lambda b, h, qi: (0,) * len(shape))
    pair = 2 * HEAD_DIM
    o_a = _attention_call(
        functools.partial(_attn_diff_kernel, lambda_init=lambda_init), "attn_diff", A_HEADS, TQ_SOFTMAX,
        qa, ka, va, pair, pair, [lam, subln_g.reshape(pair, 1)],
        [full((4, HEAD_DIM)), full((pair, 1))], 4, pair)
    o_b = _attention_call(_attn_fox_kernel, "attn_fox", B_HEADS // 2, TQ_SOFTMAX, qb, kb, vb, 2 * FOX_K, pair,
                          [], [], 4, HEAD_DIM)
    return [o_a, o_b]


def _sb_attention(qT, kT, vT):
    full = lambda shape: pl.BlockSpec(shape, lambda b, h, qi: (0,) * len(shape))
    pair = 2 * HEAD_DIM
    later = jnp.triu(jnp.ones((SB_BLOCK, SB_BLOCK), F32), 1).astype(BF16)
    return [_attention_call(_attn_sb_kernel, "attn_sb", C_HEADS // 2, TQ_STICK, qT, kT, vT, pair, pair,
                            [later], [full((SB_BLOCK, SB_BLOCK))], 2, HEAD_DIM)]


def _out_proj_route_kernel(*refs, n_parts):
    o_refs = refs[:n_parts]
    wo_refs = refs[n_parts:2 * n_parts]
    x_ref, mod_ref, g_ref, wrh_ref, wrl_ref, br_ref, x1_o, h2_o, ids_o, wts_o = refs[2 * n_parts:]
    mod_rows = mod_ref[0]
    gate1 = mod_rows[2:3]
    mix = _dot(o_refs[0][0], wo_refs[0][...])
    for o_ref, wo_ref in zip(o_refs[1:], wo_refs[1:]):
        mix = mix + _dot(o_ref[0], wo_ref[...])
    x1 = x_ref[0] + gate1 * mix
    x1_o[0] = x1
    h2 = _adaln(x1, mod_rows, g_ref[...], 3)
    h2_o[0] = h2

    h_hi, h_lo = _split2(h2)
    wr_hi = wrh_ref[...]
    logits = _dot(h_hi, wr_hi) + _dot(h_hi, wrl_ref[...]) + _dot(h_lo, wr_hi) + br_ref[...]

    lane = lax.broadcasted_iota(jnp.int32, logits.shape, 1)
    big = jnp.int32(LANES)
    g_mask = lane < N_GROUPS
    g_log = jnp.where(g_mask, logits, NEG)
    g_max = jnp.max(g_log, axis=-1, keepdims=True)
    g_sum = jnp.sum(jnp.where(g_mask, jnp.exp(g_log - g_max), 0.0), axis=-1, keepdims=True)
    g_w = 1.0 / g_sum
    g_idx = jnp.min(jnp.where(g_mask & (g_log == g_max), lane, big), axis=-1, keepdims=True)

    e_mask = (lane >= ROUTE_E0) & (lane < ROUTE_E0 + N_EXP_TOTAL) & ((lane >> 3) == g_idx + 1)
    e_log = jnp.where(e_mask, logits, NEG)
    e_max = jnp.max(e_log, axis=-1, keepdims=True)
    e_exp = jnp.where(e_mask, jnp.exp(e_log - e_max), 0.0)
    e_prob = e_exp / jnp.sum(e_exp, axis=-1, keepdims=True)
    p1 = jnp.max(jnp.where(e_mask, e_prob, -1.0), axis=-1, keepdims=True)
    i1 = jnp.min(jnp.where(e_mask & (e_prob == p1), lane, big), axis=-1, keepdims=True)
    rest = e_mask & (lane != i1)
    p2 = jnp.max(jnp.where(rest, e_prob, -1.0), axis=-1, keepdims=True)
    i2 = jnp.min(jnp.where(rest & (e_prob == p2), lane, big), axis=-1, keepdims=True)
    den = p1 + p2
    w1 = p1 / den * g_w
    w2 = p2 / den * g_w
    ids_o[0] = jnp.where(lane == 0, i1 - ROUTE_E0, jnp.where(lane == 1, i2 - ROUTE_E0, 0))
    wts_o[0] = jnp.where(lane == 0, w1, jnp.where(lane == 1, w2, 0.0))


def _out_proj_route(o_parts, x, mod, g2, w_out, w_gr, b_gr, w_er, b_er):
    bsz, seq, d = x.shape
    widths = [o.shape[-1] for o in o_parts]
    starts = np.cumsum([0] + widths)
    wo_parts = [w_out[starts[i]:starts[i + 1]].astype(BF16) for i in range(len(o_parts))]
    wr = jnp.zeros((d, LANES), F32)
    wr = wr.at[:, :N_GROUPS].set(w_gr)
    wr = wr.at[:, ROUTE_E0:ROUTE_E0 + N_EXP_TOTAL].set(
        jnp.transpose(w_er, (1, 0, 2)).reshape(d, N_EXP_TOTAL))
    br = jnp.zeros((1, LANES), F32)
    br = br.at[0, :N_GROUPS].set(b_gr)
    br = br.at[0, ROUTE_E0:ROUTE_E0 + N_EXP_TOTAL].set(b_er.reshape(N_EXP_TOTAL))
    wr_hi = wr.astype(BF16)
    wr_lo = (wr - wr_hi.astype(F32)).astype(BF16)
    full = lambda shape: pl.BlockSpec(shape, lambda b, s: (0,) * len(shape))
    tile = lambda w: pl.BlockSpec((1, TM, w), lambda b, s: (b, s, 0))
    return pl.pallas_call(
        functools.partial(_out_proj_route_kernel, n_parts=len(o_parts)),
        grid=(bsz, seq // TM),
        in_specs=[tile(w) for w in widths] + [full((w, d)) for w in widths] + [
            tile(d),
            pl.BlockSpec((1, 6, d), lambda b, s: (b, 0, 0)),
            full((1, d)), full((d, LANES)), full((d, LANES)), full((1, LANES)),
        ],
        out_specs=[tile(d), tile(d), tile(LANES), tile(LANES)],
        out_shape=[
            jax.ShapeDtypeStruct((bsz, seq, d), F32),
            jax.ShapeDtypeStruct((bsz, seq, d), F32),
            jax.ShapeDtypeStruct((bsz, seq, LANES), jnp.int32),
            jax.ShapeDtypeStruct((bsz, seq, LANES), F32),
        ],
        name="out_proj_route",
    )(*o_parts, *wo_parts, x, mod, g2.reshape(1, d), wr_hi, wr_lo, br)


def _expert_onehots(ids_ref):
    tc = ids_ref.shape[2]
    expert = lax.broadcasted_iota(jnp.int32, (N_EXP_TOTAL, tc), 0)
    ids = ids_ref[0]
    return [jnp.where(expert == ids[k:k + 1], 1.0, 0.0) for k in range(2)]


def _rank_kernel(ids_ref, rank_o, cnt_o, carry):
    tc = ids_ref.shape[2]

    @pl.when(pl.program_id(0) == 0)
    def _():
        carry[...] = jnp.zeros_like(carry)

    r = lax.broadcasted_iota(jnp.int32, (tc, tc), 0)
    c = lax.broadcasted_iota(jnp.int32, (tc, tc), 1)
    before = jnp.where(r < c, 1.0, 0.0).astype(BF16)
    base = carry[...]
    rank_o[0] = jnp.zeros(rank_o.shape[1:], rank_o.dtype)
    for k, oh in enumerate(_expert_onehots(ids_ref)):
        prefix = _dot(oh.astype(BF16), before)
        rank_o[0, k:k + 1, :] = jnp.sum(oh * (base + prefix), axis=0, keepdims=True).astype(jnp.int32)
        base = base + jnp.sum(oh, axis=-1, keepdims=True)
    carry[...] = base
    cnt_o[...] = jnp.broadcast_to(base, cnt_o.shape)


def _pos_kernel(ids_ref, rank_ref, off_ref, pos_o):
    rank = rank_ref[0]
    pos_o[0] = jnp.zeros(pos_o.shape[1:], pos_o.dtype)
    for k, oh in enumerate(_expert_onehots(ids_ref)):
        off = jnp.sum(oh * off_ref[...], axis=0, keepdims=True).astype(jnp.int32)
        pos_o[0, k:k + 1, :] = off + rank[k:k + 1]


def _dispatch_plan(ids2):
    n = ids2.shape[0]
    nc = n // RANK_TC
    ids_t = jnp.zeros((nc, 8, RANK_TC), jnp.int32).at[:, :2].set(
        jnp.transpose(ids2.reshape(nc, RANK_TC, 2), (0, 2, 1)))
    blk = pl.BlockSpec((1, 8, RANK_TC), lambda c: (c, 0, 0))
    rank, cnt = pl.pallas_call(
        _rank_kernel,
        grid=(nc,),
        in_specs=[blk],
        out_specs=[blk, pl.BlockSpec((N_EXP_TOTAL, LANES), lambda c: (0, 0))],
        out_shape=[jax.ShapeDtypeStruct((nc, 8, RANK_TC), jnp.int32),
                   jax.ShapeDtypeStruct((N_EXP_TOTAL, LANES), F32)],
        scratch_shapes=[pltpu.VMEM((N_EXP_TOTAL, 1), F32)],
        name="moe_rank",
    )(ids_t)
    counts = cnt[:, 0].astype(jnp.int32)
    tiles_per = (counts + T_EXP - 1) // T_EXP
    tile_start = jnp.cumsum(tiles_per) - tiles_per
    n_used = jnp.sum(tiles_per)
    pos = pl.pallas_call(
        _pos_kernel,
        grid=(nc,),
        in_specs=[blk, blk, pl.BlockSpec((N_EXP_TOTAL, 1), lambda c: (0, 0))],
        out_specs=blk,
        out_shape=jax.ShapeDtypeStruct((nc, 8, RANK_TC), jnp.int32),
        name="moe_pos",
    )(ids_t, rank, (tile_start * T_EXP).astype(F32).reshape(N_EXP_TOTAL, 1))
    n_tiles = (2 * n) // T_EXP + N_EXP_TOTAL
    tile_idx = jnp.minimum(jnp.arange(n_tiles, dtype=jnp.int32), n_used - 1)
    tile_expert = jnp.sum((tile_start[None, :] <= tile_idx[:, None]).astype(jnp.int32), axis=1) - 1
    per = RANK_TC // TC
    pos_chunks = jnp.transpose(pos[:, :2].reshape(nc, 2, per, TC), (0, 2, 1, 3)).reshape(n // TC, 1, 2 * TC)
    last_tile = jnp.where(tiles_per > 0, tile_start + tiles_per - 1, -1)
    return pos_chunks, tile_expert, n_used.reshape(1).astype(jnp.int32), n_tiles, last_tile


def _row_copy_wait(src, dst, sem, rows):
    pltpu.make_async_copy(src.at[pl.ds(0, rows)], dst.at[pl.ds(0, rows)], sem).wait()


def _dispatch_kernel(last_ref, nu_ref, pos_ref, h_ref, xs_hbm, zero_s, sem):
    @pl.when(pl.program_id(0) == 0)
    def _():
        zero_s[...] = jnp.zeros_like(zero_s)

        def zero_tile(j):
            return pltpu.make_async_copy(zero_s, xs_hbm.at[pl.ds(pl.multiple_of(j * T_EXP, T_EXP), T_EXP)], sem)

        for e in range(N_EXP_TOTAL):
            @pl.when(last_ref[e] >= 0)
            def _():
                zero_tile(last_ref[e]).start()
        for e in range(N_EXP_TOTAL):
            @pl.when(last_ref[e] >= 0)
            def _():
                zero_tile(0).wait()

        def unused_tile(j, carry):
            tile = zero_tile(j)
            tile.start()
            tile.wait()
            return carry

        lax.fori_loop(nu_ref[0], xs_hbm.shape[0] // T_EXP, unused_tile, 0)

    def issue(i, carry):
        for k in range(2):
            dst = pos_ref[0, 0, k * TC + i]
            pltpu.make_async_copy(h_ref.at[pl.ds(i, 1)], xs_hbm.at[pl.ds(dst, 1)], sem).start()
        return carry

    lax.fori_loop(0, TC, issue, 0, unroll=ISSUE_UNROLL)
    for k in range(2):
        _row_copy_wait(h_ref, xs_hbm, sem, TC)


def _dispatch(h2_flat, pos_chunks, last_tile, n_used, n_rows):
    n, d = h2_flat.shape
    grid_spec = pltpu.PrefetchScalarGridSpec(
        num_scalar_prefetch=2,
        grid=(n // TC,),
        in_specs=[
            pl.BlockSpec((1, 1, 2 * TC), lambda c, pad, nu: (c, 0, 0), memory_space=pltpu.SMEM),
            pl.BlockSpec((TC, d), lambda c, pad, nu: (c, 0)),
        ],
        out_specs=pl.BlockSpec(memory_space=pl.ANY),
        scratch_shapes=[pltpu.VMEM((T_EXP, d), h2_flat.dtype), pltpu.SemaphoreType.DMA(())],
    )
    return pl.pallas_call(
        _dispatch_kernel,
        grid_spec=grid_spec,
        out_shape=jax.ShapeDtypeStruct((n_rows, d), h2_flat.dtype),
        name="moe_dispatch",
    )(last_tile, n_used, pos_chunks, h2_flat)


def _expert_kernel(te_ref, nu_ref, xs_ref, w1_ref, w3_ref, w2_ref, ys_ref, w1_s, w3_s, w2_s):
    j = pl.program_id(0)
    prev = te_ref[jnp.maximum(j - 1, 0)]

    @pl.when((j == 0) | (te_ref[j] != prev))
    def _():
        w1_s[...] = w1_ref[0].astype(BF16)
        w3_s[...] = w3_ref[0].astype(BF16)
        w2_s[...] = w2_ref[0].astype(BF16)

    @pl.when(j < nu_ref[0])
    def _():
        xb = xs_ref[...].astype(BF16)
        a = _dot(xb, w1_s[...])
        b = _dot(xb, w3_s[...])
        hid = a * jax.nn.sigmoid(a) * b
        ys_ref[...] = _dot(hid.astype(BF16), w2_s[...])

    @pl.when(j >= nu_ref[0])
    def _():
        ys_ref[...] = jnp.zeros_like(ys_ref)


def _experts(xs, n_tiles, tile_expert, n_used, w1, w3, w2, layer):
    d = xs.shape[1]
    n_rows = n_tiles * T_EXP
    f = w1.shape[-1]
    n_all = w1.shape[0] * N_EXP_TOTAL
    tile_expert = tile_expert + layer * N_EXP_TOTAL
    grid_spec = pltpu.PrefetchScalarGridSpec(
        num_scalar_prefetch=2,
        grid=(n_tiles,),
        in_specs=[
            pl.BlockSpec((T_EXP, d), lambda j, te, nu: (jnp.minimum(j, nu[0] - 1), 0)),
            pl.BlockSpec((1, d, f), lambda j, te, nu: (te[j], 0, 0)),
            pl.BlockSpec((1, d, f), lambda j, te, nu: (te[j], 0, 0)),
            pl.BlockSpec((1, f, d), lambda j, te, nu: (te[j], 0, 0)),
        ],
        out_specs=pl.BlockSpec((T_EXP, d), lambda j, te, nu: (j, 0)),
        scratch_shapes=[pltpu.VMEM((d, f), BF16), pltpu.VMEM((d, f), BF16), pltpu.VMEM((f, d), BF16)],
    )
    return pl.pallas_call(
        _expert_kernel,
        grid_spec=grid_spec,
        out_shape=jax.ShapeDtypeStruct((n_rows, d), F32),
        name="moe_experts",
    )(tile_expert, n_used, xs, w1.reshape(n_all, d, f), w3.reshape(n_all, d, f), w2.reshape(n_all, f, d))


def _combine_kernel(pos_ref, ys_hbm, x1_ref, wts_ref, mod_ref, x2_o, buf, sem):
    def issue(i, carry):
        for k in range(2):
            src = pos_ref[0, 0, k * TC + i]
            pltpu.make_async_copy(ys_hbm.at[pl.ds(src, 1)], buf.at[k, pl.ds(i, 1)], sem).start()
        return carry

    lax.fori_loop(0, TC, issue, 0, unroll=ISSUE_UNROLL)
    for k in range(2):
        _row_copy_wait(ys_hbm, buf.at[k], sem, TC)
    wts = wts_ref[0]
    y = wts[:, 0:1] * buf[0] + wts[:, 1:2] * buf[1]
    gate2 = mod_ref[0][5:6]
    x2_o[0] = x1_ref[0] + gate2 * y


def _combine(ys, pos_chunks, x1, wts, mod):
    bsz, seq, d = x1.shape
    per_b = seq // TC
    return pl.pallas_call(
        _combine_kernel,
        grid=(bsz, per_b),
        in_specs=[
            pl.BlockSpec((1, 1, 2 * TC), lambda b, s: (b * per_b + s, 0, 0), memory_space=pltpu.SMEM),
            pl.BlockSpec(memory_space=pl.ANY),
            pl.BlockSpec((1, TC, d), lambda b, s: (b, s, 0)),
            pl.BlockSpec((1, TC, LANES), lambda b, s: (b, s, 0)),
            pl.BlockSpec((1, 6, d), lambda b, s: (b, 0, 0)),
        ],
        out_specs=pl.BlockSpec((1, TC, d), lambda b, s: (b, s, 0)),
        out_shape=jax.ShapeDtypeStruct((bsz, seq, d), F32),
        scratch_shapes=[pltpu.VMEM((2, TC, d), F32), pltpu.SemaphoreType.DMA(())],
        name="moe_combine",
    )(pos_chunks, ys, x1, wts, mod)


def _moe(h2, ids, wts, x1, mod, w1, w3, w2, layer):
    bsz, seq, d = h2.shape
    n = bsz * seq
    pos_chunks, tile_expert, n_used, n_tiles, last_tile = _dispatch_plan(ids.reshape(n, LANES)[:, :2])
    xs = _dispatch(h2.reshape(n, d), pos_chunks, last_tile, n_used, n_tiles * T_EXP)
    ys = _experts(xs, n_tiles, tile_expert, n_used, w1, w3, w2, layer)
    return _combine(ys, pos_chunks, x1, wts, mod)


def kernel(x, c, positions, mod_w, mod_b, norm1_g, norm2_g, ev_w_in, ev_b_f, ev_qn_a, ev_kn_a, ev_lam,
           ev_subln_g, ev_qn_b, ev_kn_b, ev_w_out, od_w_in, od_w_out, moe_w_gr, moe_b_gr, moe_w_er,
           moe_b_er, moe_w1, moe_w3, moe_w2):
    depth = mod_w.shape[0]
    mod = _modulation(c, mod_w, mod_b)
    cos_t, sin_t = _rope_tables(positions)
    for l in range(depth):
        if l % 2 == 0:
            e = l // 2
            lambda_init = 0.8 - 0.6 * math.exp(-0.3 * l)
            qa, ka, va, qb, kb, vb = _ln_proj_even(
                x, mod[l], norm1_g[l], ev_w_in[e], ev_b_f[e], ev_qn_a[e], ev_kn_a[e], ev_qn_b[e],
                ev_kn_b[e], cos_t, sin_t)
            o = _even_mixer_attention(qa, ka, va, qb, kb, vb, ev_lam[e], ev_subln_g[e], lambda_init)
            w_out = ev_w_out[e]
        else:
            od = l // 2
            q, k, v = _ln_proj_odd(x, mod[l], norm1_g[l], od_w_in[od])
            o = _sb_attention(q, k, v)
            w_out = od_w_out[od]
        x1, h2, ids, wts = _out_proj_route(o, x, mod[l], norm2_g[l], w_out, moe_w_gr[l], moe_b_gr[l],
                                           moe_w_er[l], moe_b_er[l])
        x = _moe(h2, ids, wts, x1, mod[l], moe_w1, moe_w3, moe_w2, l)
    return x
```

```python
import functools
import math

import numpy as np
import jax
import jax.numpy as jnp
from jax import lax
from jax.experimental import pallas as pl
from jax.experimental.pallas import tpu as pltpu

F32 = jnp.float32
BF16 = jnp.bfloat16

D_MODEL = 1024
HEAD_DIM = 64
HALF = HEAD_DIM // 2
A_HEADS = 4
B_HEADS = 8
C_HEADS = 16
SEC = 512
N_GROUPS = 4
N_EXPERTS = 8
N_EXP_TOTAL = N_GROUPS * N_EXPERTS
ROPE_THETA = 10000.0
RMS_EPS = 1e-6
QK_SCALE = HEAD_DIM ** -0.5
LOG2E = math.log2(math.e)

LANES = 128
TM = 1024
TQ_SOFTMAX = 4096
TQ_STICK = 1024
TK_SOFTMAX = 1024
TK_STICK = 512
SB_BLOCK = 256
SB_HEAD = 256
DEAD_BITS = 160.0
FOX_AUG = 16
FOX_K = HEAD_DIM + FOX_AUG
T_EXP = 512
TC = 1024
RANK_TC = 1024
ISSUE_UNROLL = 32
NEG = -1e30
ROUTE_E0 = 8

NT_DIMS = (((1,), (1,)), ((), ()))
TN_DIMS = (((0,), (0,)), ((), ()))


def _dot(a, b):
    return jnp.dot(a, b, preferred_element_type=F32)


def _dot_nt(a, b):
    return lax.dot_general(a, b, NT_DIMS, preferred_element_type=F32)


def _dot_tn(a, b):
    return lax.dot_general(a, b, TN_DIMS, preferred_element_type=F32)


def _split2(x):
    hi = x.astype(BF16)
    lo = (x - hi.astype(F32)).astype(BF16)
    return hi, lo


def _split3(x):
    hi = x.astype(BF16)
    r = x - hi.astype(F32)
    mid = r.astype(BF16)
    lo = (r - mid.astype(F32)).astype(BF16)
    return hi, mid, lo


def _softplus_neg_abs(z):
    return jnp.log(1.0 + jnp.exp(-jnp.abs(z)))


def _log_sigmoid(z):
    return jnp.minimum(z, 0.0) - _softplus_neg_abs(z)


def _rms_rows(x, eps=RMS_EPS):
    return x * lax.rsqrt(jnp.mean(x * x, axis=-1, keepdims=True) + eps)


def _mod_kernel(c_ref, w_ref, b_ref, o_ref):
    c = c_ref[...]
    ca = c * jax.nn.sigmoid(c)
    c_hi, c_mid, c_lo = _split3(ca)
    w = w_ref[0]
    w_hi, w_lo = _split2(w)
    acc = _dot(c_hi, w_hi) + _dot(c_hi, w_lo) + _dot(c_mid, w_hi) + _dot(c_lo, w_hi) + _dot(c_mid, w_lo)
    o_ref[0] = acc + b_ref[0]


def _modulation(c, mod_w, mod_b):
    depth, d, n6 = mod_w.shape
    bsz = c.shape[0]
    rows = 8
    tn = 1536
    c_pad = jnp.zeros((rows, d), F32).at[:bsz].set(c)
    out = pl.pallas_call(
        _mod_kernel,
        grid=(depth, n6 // tn),
        in_specs=[
            pl.BlockSpec((rows, d), lambda l, j: (0, 0)),
            pl.BlockSpec((1, d, tn), lambda l, j: (l, 0, j)),
            pl.BlockSpec((1, 1, tn), lambda l, j: (l, 0, j)),
        ],
        out_specs=pl.BlockSpec((1, rows, tn), lambda l, j: (l, 0, j)),
        out_shape=jax.ShapeDtypeStruct((depth, rows, n6), F32),
        name="adaln_mod",
    )(c_pad, mod_w, mod_b.reshape(depth, 1, n6))
    return out[:, :bsz].reshape(depth, bsz, 6, d)


def _rope_table_kernel(pos_ref, inv_ref, cos_ref, sin_ref):
    ang = pos_ref[0].astype(F32) * inv_ref[...]
    cos_ref[0] = jnp.cos(ang)
    sin_ref[0] = jnp.sin(ang)


def _rope_tables(positions):
    bsz, seq = positions.shape
    ts = 2048
    inv = ROPE_THETA ** (-2.0 * jnp.arange(HALF, dtype=F32) / HEAD_DIM)
    return pl.pallas_call(
        _rope_table_kernel,
        grid=(bsz, seq // ts),
        in_specs=[
            pl.BlockSpec((1, 1, ts), lambda b, s: (b, 0, s)),
            pl.BlockSpec((HALF, 1), lambda b, s: (0, 0)),
        ],
        out_specs=[pl.BlockSpec((1, HALF, ts), lambda b, s: (b, 0, s))] * 2,
        out_shape=[jax.ShapeDtypeStruct((bsz, HALF, seq), F32)] * 2,
        name="rope_tables",
    )(positions.reshape(bsz, 1, seq), inv.reshape(HALF, 1))


def _adaln(x, mod_rows, g, first):
    shift = mod_rows[first:first + 1]
    scale = mod_rows[first + 1:first + 2]
    return _rms_rows(x) * g * (1.0 + scale) + shift


def _ln_proj_even_kernel(x_ref, mod_ref, g_ref, wT_ref, wfT_ref, bf_ref,
                         qna_ref, kna_ref, qnb_ref, knb_ref, cos_ref, sin_ref,
                         qa_o, ka_o, va_o, qb_o, kb_o, vb_o, carry):
    tm = x_ref.shape[1]

    @pl.when(pl.program_id(1) == 0)
    def _():
        carry[...] = jnp.zeros_like(carry)

    hb = _adaln(x_ref[0], mod_ref[0], g_ref[...], 0).astype(BF16)
    cos = cos_ref[0]
    sin = sin_ref[0]

    def section(idx):
        return _dot_nt(wT_ref[idx * SEC:(idx + 1) * SEC, :], hb)

    def norm_heads(p, g_col, out_ref, rope, scale, stride):
        for j in range(SEC // HEAD_DIM):
            xj = p[j * HEAD_DIM:(j + 1) * HEAD_DIM]
            yj = xj * lax.rsqrt(jnp.mean(xj * xj, axis=0, keepdims=True) + RMS_EPS) * g_col
            if rope:
                y1 = yj[:HALF]
                y2 = yj[HALF:]
                yj = jnp.concatenate([y1 * cos - y2 * sin, y2 * cos + y1 * sin], axis=0)
            out_ref[0, j * stride:j * stride + HEAD_DIM, :] = (yj * scale).astype(out_ref.dtype)

    softmax_q_scale = QK_SCALE * LOG2E
    norm_heads(section(0), qna_ref[...], qa_o, True, softmax_q_scale, HEAD_DIM)
    norm_heads(section(1), kna_ref[...], ka_o, True, 1.0, HEAD_DIM)
    va_o[0] = section(2).astype(va_o.dtype)
    norm_heads(section(3), qnb_ref[...], qb_o, False, softmax_q_scale, FOX_K)
    norm_heads(section(4), knb_ref[...], kb_o, False, 1.0, FOX_K)
    vb_o[0] = section(5).astype(vb_o.dtype)

    r = lax.broadcasted_iota(jnp.int32, (tm, tm), 0)
    c = lax.broadcasted_iota(jnp.int32, (tm, tm), 1)
    upto = jnp.where(r <= c, 1.0, 0.0).astype(BF16)
    log_f = _log_sigmoid(_dot_nt(wfT_ref[...], hb)[:B_HEADS] + bf_ref[...])
    f_hi, f_mid, f_lo = _split3(log_f)
    cum = _dot(f_hi, upto) + _dot(f_mid, upto) + _dot(f_lo, upto) + carry[...]
    carry[...] = cum[:, tm - 1:tm]

    row = lax.broadcasted_iota(jnp.int32, (FOX_AUG, tm), 0)
    for j in range(B_HEADS):
        pieces = [p.astype(F32) for p in _split3(cum[j:j + 1] * LOG2E)]
        q_aug = jnp.where(row < 3, 1.0, 0.0)
        k_aug = jnp.where((row >= 3) & (row < 6), 1.0, 0.0)
        for i, piece in enumerate(pieces):
            q_aug = jnp.where(row == 3 + i, piece, q_aug)
            k_aug = jnp.where(row == i, -piece, k_aug)
        lo = j * FOX_K + HEAD_DIM
        qb_o[0, lo:lo + FOX_AUG, :] = q_aug.astype(qb_o.dtype)
        kb_o[0, lo:lo + FOX_AUG, :] = k_aug.astype(kb_o.dtype)


def _ln_proj_even(x, mod, g, w_in, b_f, qn_a, kn_a, qn_b, kn_b, cos_t, sin_t):
    bsz, seq, d = x.shape
    n_main = 6 * SEC
    wT = w_in[:, :n_main].T.astype(BF16)
    wfT = jnp.zeros((16, d), F32).at[:B_HEADS].set(w_in[:, n_main:].T).astype(BF16)
    col = lambda v: v.reshape(HEAD_DIM, 1)
    full = lambda shape: pl.BlockSpec(shape, lambda b, s: (0,) * len(shape))
    rows_spec = lambda rows: pl.BlockSpec((1, rows, TM), lambda b, s: (b, 0, s))
    rows_shape = lambda rows: jax.ShapeDtypeStruct((bsz, rows, seq), BF16)
    out_rows = [SEC, SEC, SEC, B_HEADS * FOX_K, B_HEADS * FOX_K, SEC]
    return pl.pallas_call(
        _ln_proj_even_kernel,
        grid=(bsz, seq // TM),
        in_specs=[
            pl.BlockSpec((1, TM, d), lambda b, s: (b, s, 0)),
            pl.BlockSpec((1, 6, d), lambda b, s: (b, 0, 0)),
            full((1, d)), full((n_main, d)), full((16, d)), full((B_HEADS, 1)),
            full((HEAD_DIM, 1)), full((HEAD_DIM, 1)), full((HEAD_DIM, 1)), full((HEAD_DIM, 1)),
            rows_spec(HALF), rows_spec(HALF),
        ],
        out_specs=[rows_spec(n) for n in out_rows],
        out_shape=[rows_shape(n) for n in out_rows],
        scratch_shapes=[pltpu.VMEM((B_HEADS, 1), F32)],
        name="ln_proj_even",
    )(x, mod, g.reshape(1, d), wT, wfT, b_f.reshape(B_HEADS, 1), col(qn_a), col(kn_a), col(qn_b),
      col(kn_b), cos_t, sin_t)


def _ln_proj_odd_kernel(x_ref, mod_ref, g_ref, wT_ref, q_o, k_o, v_o):
    hb = _adaln(x_ref[0], mod_ref[0], g_ref[...], 0).astype(BF16)
    width = q_o.shape[1]
    for idx, (out_ref, scale) in enumerate(((q_o, QK_SCALE * LOG2E), (k_o, 1.0), (v_o, 1.0))):
        for half in range(2):
            lo = idx * width + half * (width // 2)
            p = _dot_nt(wT_ref[lo:lo + width // 2, :], hb)
            out_ref[0, half * (width // 2):(half + 1) * (width // 2), :] = (p * scale).astype(out_ref.dtype)


def _ln_proj_odd(x, mod, g, w_in):
    bsz, seq, d = x.shape
    width = C_HEADS * HEAD_DIM
    wT = w_in.T.astype(BF16)
    full = lambda shape: pl.BlockSpec(shape, lambda b, s: (0,) * len(shape))
    spec = pl.BlockSpec((1, width, TM), lambda b, s: (b, 0, s))
    shape = jax.ShapeDtypeStruct((bsz, width, seq), BF16)
    return pl.pallas_call(
        _ln_proj_odd_kernel,
        grid=(bsz, seq // TM),
        in_specs=[
            pl.BlockSpec((1, TM, d), lambda b, s: (b, s, 0)),
            pl.BlockSpec((1, 6, d), lambda b, s: (b, 0, 0)),
            full((1, d)), full((3 * width, d)),
        ],
        out_specs=[spec] * 3,
        out_shape=[shape] * 3,
        name="ln_proj_odd",
    )(x, mod, g.reshape(1, d), wT)


def _diag_valid(shape, strict):
    key = lax.broadcasted_iota(jnp.int32, shape, 0)
    qry = lax.broadcasted_iota(jnp.int32, shape, 1)
    return (qry > key) if strict else (qry >= key)


def _mask_triangle(x, strict, fill):
    tk = x.shape[0]
    square = jnp.where(_diag_valid((tk, tk), strict), x[:, :tk], fill)
    return square if x.shape[1] == tk else jnp.concatenate([square, x[:, tk:]], axis=1)


def _softmax_tile(kT, qT_ref, q_rows, vT, m_ref, l_ref, acc_ref, diag):
    qs = slice(0 if diag is None else diag, None)
    sT = _dot_tn(kT, qT_ref[0, q_rows, qs])
    if diag is not None:
        sT = _mask_triangle(sT, False, NEG)
    m_prev = m_ref[:, qs]
    m_new = jnp.maximum(m_prev, jnp.max(sT, axis=0, keepdims=True))
    alpha = jnp.exp2(m_prev - m_new)
    p = jnp.exp2(sT - m_new)
    l_ref[:, qs] = alpha * l_ref[:, qs] + jnp.sum(p, axis=0, keepdims=True)
    acc_ref[:, qs] = alpha * acc_ref[:, qs] + _dot(vT, p.astype(BF16))
    m_ref[:, qs] = m_new


def _init_softmax(m_refs, l_refs, acc_refs):
    for r in m_refs:
        r[...] = jnp.full_like(r, -jnp.inf)
    for r in l_refs + acc_refs:
        r[...] = jnp.zeros_like(r)


def _key_tile(ki, tk):
    return pl.ds(pl.multiple_of(ki * tk, tk), tk)


def _attn_diff_kernel(qT_ref, kT_ref, vT_ref, lam_ref, sg_ref, o_ref,
                      m0, m1, l0, l1, acc0, acc1, *, lambda_init):
    qi = pl.program_id(2)
    ms, ls, accs = [m0, m1], [l0, l1], [acc0, acc1]
    _init_softmax(ms, ls, accs)

    def tiles(ki, diag):
        cols = _key_tile(ki, TK_SOFTMAX)
        for c in range(2):
            hs = slice(c * HEAD_DIM, (c + 1) * HEAD_DIM)
            _softmax_tile(kT_ref[0, hs, cols], qT_ref, hs, vT_ref[0, :, cols], ms[c], ls[c], accs[c], diag)

    def below_diagonal(ki, carry):
        tiles(ki, None)
        return carry

    diag_tiles = qT_ref.shape[2] // TK_SOFTMAX
    n_below = qi * diag_tiles
    lax.fori_loop(0, n_below, below_diagonal, 0)
    for d in range(diag_tiles):
        tiles(n_below + d, d * TK_SOFTMAX)
    lam = lam_ref[...]
    e1 = jnp.exp(jnp.sum(lam[0:1] * lam[1:2], axis=-1, keepdims=True))
    e2 = jnp.exp(jnp.sum(lam[2:3] * lam[3:4], axis=-1, keepdims=True))
    lam_full = e1 - e2 + lambda_init
    oT = acc0[...] * (1.0 / l0[...]) - lam_full * (acc1[...] * (1.0 / l1[...]))
    oT = oT * lax.rsqrt(jnp.mean(oT * oT, axis=0, keepdims=True) + RMS_EPS)
    oT = oT * sg_ref[...] * (1.0 - lambda_init)
    o_ref[0] = oT.T.astype(o_ref.dtype)


def _attn_fox_kernel(qT_ref, kT_ref, vT_ref, o_ref, m0, m1, l0, l1, acc0, acc1):
    qi = pl.program_id(2)
    ms, ls, accs = [m0, m1], [l0, l1], [acc0, acc1]
    _init_softmax(ms, ls, accs)

    def tiles(ki, diag):
        cols = _key_tile(ki, TK_SOFTMAX)
        for j in range(2):
            ks = slice(j * FOX_K, (j + 1) * FOX_K)
            vs = slice(j * HEAD_DIM, (j + 1) * HEAD_DIM)
            _softmax_tile(kT_ref[0, ks, cols], qT_ref, ks, vT_ref[0, vs, cols], ms[j], ls[j], accs[j], diag)

    def below_diagonal(ki, carry):
        tiles(ki, None)
        return carry

    diag_tiles = qT_ref.shape[2] // TK_SOFTMAX
    n_below = qi * diag_tiles
    lax.fori_loop(0, n_below, below_diagonal, 0)
    for d in range(diag_tiles):
        tiles(n_below + d, d * TK_SOFTMAX)
    oT = jnp.concatenate([acc0[...] * (1.0 / l0[...]), acc1[...] * (1.0 / l1[...])], axis=0)
    o_ref[0] = oT.T.astype(o_ref.dtype)


def _sb_tile(kT, qT_ref, q_rows, vT, later_ref, r_ref, acc_ref, diag, q_stop=None):
    qs = slice(0 if diag is None else diag, q_stop)
    zT = _dot_tn(kT, qT_ref[0, q_rows, qs])
    neg_keep = jnp.maximum(zT, 0.0) + jnp.log2(1.0 + jnp.exp2(-jnp.abs(zT)))
    log_beta = zT - neg_keep
    if diag is not None:
        neg_keep = _mask_triangle(neg_keep, True, 0.0)
    keep_bf = neg_keep.astype(BF16)
    beyond = r_ref[:, qs]
    parts = []
    for blk in reversed(range(zT.shape[0] // SB_BLOCK)):
        rows = slice(blk * SB_BLOCK, (blk + 1) * SB_BLOCK)
        within = _dot(later_ref[...], keep_bf[rows])
        parts.append(within + beyond)
        beyond = beyond + within[0:1] + neg_keep[blk * SB_BLOCK:blk * SB_BLOCK + 1]
    laterT = jnp.concatenate(parts[::-1], axis=0)
    a = jnp.exp2(log_beta - laterT)
    if diag is not None:
        a = _mask_triangle(a, True, 0.0)
    acc_ref[:, qs] = acc_ref[:, qs] + _dot(vT, a.astype(BF16))
    r_ref[:, qs] = beyond


def _attn_sb_kernel(qT_ref, kT_ref, vT_ref, later_ref, o_ref, r0, r1, acc0, acc1):
    qi = pl.program_id(2)
    rs, accs = [r0, r1], [acc0, acc1]
    for r in rs + accs:
        r[...] = jnp.zeros_like(r)

    def tiles(ki, diag, q_stop=None):
        cols = _key_tile(ki, TK_STICK)
        for j in range(2):
            hs = slice(j * HEAD_DIM, (j + 1) * HEAD_DIM)
            _sb_tile(kT_ref[0, hs, cols], qT_ref, hs, vT_ref[0, hs, cols], later_ref, rs[j], accs[j], diag,
                     q_stop)

    tq = qT_ref.shape[2]
    diag_tiles = tq // TK_STICK
    n_below = qi * diag_tiles
    head = min(SB_HEAD, tq)

    def least_decay(lo, hi):
        return jnp.minimum(jnp.min(r0[:, lo:hi]), jnp.min(r1[:, lo:hi]))

    def more_to_do(state):
        i, head_decay, rest_decay = state
        return (i < n_below) & (jnp.minimum(head_decay, rest_decay) < DEAD_BITS)

    def below_diagonal(state):
        i, _, rest_decay = state
        ki = n_below - 1 - i
        if head < tq:
            @pl.when(rest_decay >= DEAD_BITS)
            def _():
                tiles(ki, None, head)

            @pl.when(rest_decay < DEAD_BITS)
            def _():
                tiles(ki, None)
        else:
            tiles(ki, None)
        return i + 1, least_decay(0, head), (least_decay(head, tq) if head < tq else jnp.float32(DEAD_BITS))

    for d in reversed(range(diag_tiles)):
        tiles(n_below + d, d * TK_STICK)
    start = (0, least_decay(0, head), least_decay(head, tq) if head < tq else jnp.float32(DEAD_BITS))
    lax.while_loop(more_to_do, below_diagonal, start)
    o_ref[0] = jnp.concatenate([acc0[...], acc1[...]], axis=0).T.astype(o_ref.dtype)


def _attention_call(body, name, n_groups, tq, qT, kT, vT, qk_rows, v_rows, extra_in, extra_specs, n_stats,
                    v_dim):
    bsz, _, seq = qT.shape
    in_specs = [
        pl.BlockSpec((1, qk_rows, tq), lambda b, h, qi: (b, h, qi)),
        pl.BlockSpec((1, qk_rows, seq), lambda b, h, qi: (b, h, 0)),
        pl.BlockSpec((1, v_rows, seq), lambda b, h, qi: (b, h, 0)),
    ] + extra_specs
    scratch = [pltpu.VMEM((1, tq), F32)] * n_stats + [pltpu.VMEM((v_dim, tq), F32)] * 2
    return pl.pallas_call(
        body,
        grid=(bsz, n_groups, seq // tq),
        in_specs=in_specs,
        out_specs=pl.BlockSpec((1, tq, 2 * HEAD_DIM), lambda b, h, qi: (b, qi, h)),
        out_shape=jax.ShapeDtypeStruct((bsz, seq, n_groups * 2 * HEAD_DIM), BF16),
        scratch_shapes=scratch,
        name=name,
    )(qT, kT, vT, *extra_in)


def _even_mixer_attention(qa, ka, va, qb, kb, vb, lam, subln_g, lambda_init):
    full = lambda shape: pl.BlockSpec(shape, lambda b, h, qi: (0,) * len(shape))
    pair = 2 * HEAD_DIM
    o_a = _attention_call(
        functools.partial(_attn_diff_kernel, lambda_init=lambda_init), "attn_diff", A_HEADS, TQ_SOFTMAX,
        qa, ka, va, pair, pair, [lam, subln_g.reshape(pair, 1)],
        [full((4, HEAD_DIM)), full((pair, 1))], 4, pair)
    o_b = _attention_call(_attn_fox_kernel, "attn_fox", B_HEADS // 2, TQ_SOFTMAX, qb, kb, vb, 2 * FOX_K, pair,
                          [], [], 4, HEAD_DIM)
    return [o_a, o_b]


def _sb_attention(qT, kT, vT):
    full = lambda shape: pl.BlockSpec(shape, lambda b, h, qi: (0,) * len(shape))
    pair = 2 * HEAD_DIM
    later = jnp.triu(jnp.ones((SB_BLOCK, SB_BLOCK), F32), 1).astype(BF16)
    return [_attention_call(_attn_sb_kernel, "attn_sb", C_HEADS // 2, TQ_STICK, qT, kT, vT, pair, pair,
                            [later], [full((SB_BLOCK, SB_BLOCK))], 2, HEAD_DIM)]


def _out_proj_route_kernel(*refs, n_parts):
    o_refs = refs[:n_parts]
    wo_refs = refs[n_parts:2 * n_parts]
    x_ref, mod_ref, g_ref, wrh_ref, wrl_ref, br_ref, x1_o, h2_o, ids_o, wts_o = refs[2 * n_parts:]
    mod_rows = mod_ref[0]
    gate1 = mod_rows[2:3]
    mix = _dot(o_refs[0][0], wo_refs[0][...])
    for o_ref, wo_ref in zip(o_refs[1:], wo_refs[1:]):
        mix = mix + _dot(o_ref[0], wo_ref[...])
    x1 = x_ref[0] + gate1 * mix
    x1_o[0] = x1
    h2 = _adaln(x1, mod_rows, g_ref[...], 3)
    h2_o[0] = h2

    h_hi, h_lo = _split2(h2)
    wr_hi = wrh_ref[...]
    logits = _dot(h_hi, wr_hi) + _dot(h_hi, wrl_ref[...]) + _dot(h_lo, wr_hi) + br_ref[...]

    lane = lax.broadcasted_iota(jnp.int32, logits.shape, 1)
    big = jnp.int32(LANES)
    g_mask = lane < N_GROUPS
    g_log = jnp.where(g_mask, logits, NEG)
    g_max = jnp.max(g_log, axis=-1, keepdims=True)
    g_sum = jnp.sum(jnp.where(g_mask, jnp.exp(g_log - g_max), 0.0), axis=-1, keepdims=True)
    g_w = 1.0 / g_sum
    g_idx = jnp.min(jnp.where(g_mask & (g_log == g_max), lane, big), axis=-1, keepdims=True)

    e_mask = (lane >= ROUTE_E0) & (lane < ROUTE_E0 + N_EXP_TOTAL) & ((lane >> 3) == g_idx + 1)
    e_log = jnp.where(e_mask, logits, NEG)
    e_max = jnp.max(e_log, axis=-1, keepdims=True)
    e_exp = jnp.where(e_mask, jnp.exp(e_log - e_max), 0.0)
    e_prob = e_exp / jnp.sum(e_exp, axis=-1, keepdims=True)
    p1 = jnp.max(jnp.where(e_mask, e_prob, -1.0), axis=-1, keepdims=True)
    i1 = jnp.min(jnp.where(e_mask & (e_prob == p1), lane, big), axis=-1, keepdims=True)
    rest = e_mask & (lane != i1)
    p2 = jnp.max(jnp.where(rest, e_prob, -1.0), axis=-1, keepdims=True)
    i2 = jnp.min(jnp.where(rest & (e_prob == p2), lane, big), axis=-1, keepdims=True)
    den = p1 + p2
    w1 = p1 / den * g_w
    w2 = p2 / den * g_w
    ids_o[0] = jnp.where(lane == 0, i1 - ROUTE_E0, jnp.where(lane == 1, i2 - ROUTE_E0, 0))
    wts_o[0] = jnp.where(lane == 0, w1, jnp.where(lane == 1, w2, 0.0))


def _out_proj_route(o_parts, x, mod, g2, w_out, w_gr, b_gr, w_er, b_er):
    bsz, seq, d = x.shape
    widths = [o.shape[-1] for o in o_parts]
    starts = np.cumsum([0] + widths)
    wo_parts = [w_out[starts[i]:starts[i + 1]].astype(BF16) for i in range(len(o_parts))]
    wr = jnp.zeros((d, LANES), F32)
    wr = wr.at[:, :N_GROUPS].set(w_gr)
    wr = wr.at[:, ROUTE_E0:ROUTE_E0 + N_EXP_TOTAL].set(
        jnp.transpose(w_er, (1, 0, 2)).reshape(d, N_EXP_TOTAL))
    br = jnp.zeros((1, LANES), F32)
    br = br.at[0, :N_GROUPS].set(b_gr)
    br = br.at[0, ROUTE_E0:ROUTE_E0 + N_EXP_TOTAL].set(b_er.reshape(N_EXP_TOTAL))
    wr_hi = wr.astype(BF16)
    wr_lo = (wr - wr_hi.astype(F32)).astype(BF16)
    full = lambda shape: pl.BlockSpec(shape, lambda b, s: (0,) * len(shape))
    tile = lambda w: pl.BlockSpec((1, TM, w), lambda b, s: (b, s, 0))
    return pl.pallas_call(
        functools.partial(_out_proj_route_kernel, n_parts=len(o_parts)),
        grid=(bsz, seq // TM),
        in_specs=[tile(w) for w in widths] + [full((w, d)) for w in widths] + [
            tile(d),
            pl.BlockSpec((1, 6, d), lambda b, s: (b, 0, 0)),
            full((1, d)), full((d, LANES)), full((d, LANES)), full((1, LANES)),
        ],
        out_specs=[tile(d), tile(d), tile(LANES), tile(LANES)],
        out_shape=[
            jax.ShapeDtypeStruct((bsz, seq, d), F32),
            jax.ShapeDtypeStruct((bsz, seq, d), F32),
            jax.ShapeDtypeStruct((bsz, seq, LANES), jnp.int32),
            jax.ShapeDtypeStruct((bsz, seq, LANES), F32),
        ],
        name="out_proj_route",
    )(*o_parts, *wo_parts, x, mod, g2.reshape(1, d), wr_hi, wr_lo, br)


def _expert_onehots(ids_ref):
    tc = ids_ref.shape[2]
    expert = lax.broadcasted_iota(jnp.int32, (N_EXP_TOTAL, tc), 0)
    ids = ids_ref[0]
    return [jnp.where(expert == ids[k:k + 1], 1.0, 0.0) for k in range(2)]


def _rank_kernel(ids_ref, rank_o, cnt_o, carry):
    tc = ids_ref.shape[2]

    @pl.when(pl.program_id(0) == 0)
    def _():
        carry[...] = jnp.zeros_like(carry)

    r = lax.broadcasted_iota(jnp.int32, (tc, tc), 0)
    c = lax.broadcasted_iota(jnp.int32, (tc, tc), 1)
    before = jnp.where(r < c, 1.0, 0.0).astype(BF16)
    base = carry[...]
    rank_o[0] = jnp.zeros(rank_o.shape[1:], rank_o.dtype)
    for k, oh in enumerate(_expert_onehots(ids_ref)):
        prefix = _dot(oh.astype(BF16), before)
        rank_o[0, k:k + 1, :] = jnp.sum(oh * (base + prefix), axis=0, keepdims=True).astype(jnp.int32)
        base = base + jnp.sum(oh, axis=-1, keepdims=True)
    carry[...] = base
    cnt_o[...] = jnp.broadcast_to(base, cnt_o.shape)


def _pos_kernel(ids_ref, rank_ref, off_ref, pos_o):
    rank = rank_ref[0]
    pos_o[0] = jnp.zeros(pos_o.shape[1:], pos_o.dtype)
    for k, oh in enumerate(_expert_onehots(ids_ref)):
        off = jnp.sum(oh * off_ref[...], axis=0, keepdims=True).astype(jnp.int32)
        pos_o[0, k:k + 1, :] = off + rank[k:k + 1]


def _dispatch_plan(ids2):
    n = ids2.shape[0]
    nc = n // RANK_TC
    ids_t = jnp.zeros((nc, 8, RANK_TC), jnp.int32).at[:, :2].set(
        jnp.transpose(ids2.reshape(nc, RANK_TC, 2), (0, 2, 1)))
    blk = pl.BlockSpec((1, 8, RANK_TC), lambda c: (c, 0, 0))
    rank, cnt = pl.pallas_call(
        _rank_kernel,
        grid=(nc,),
        in_specs=[blk],
        out_specs=[blk, pl.BlockSpec((N_EXP_TOTAL, LANES), lambda c: (0, 0))],
        out_shape=[jax.ShapeDtypeStruct((nc, 8, RANK_TC), jnp.int32),
                   jax.ShapeDtypeStruct((N_EXP_TOTAL, LANES), F32)],
        scratch_shapes=[pltpu.VMEM((N_EXP_TOTAL, 1), F32)],
        name="moe_rank",
    )(ids_t)
    counts = cnt[:, 0].astype(jnp.int32)
    tiles_per = (counts + T_EXP - 1) // T_EXP
    tile_start = jnp.cumsum(tiles_per) - tiles_per
    n_used = jnp.sum(tiles_per)
    pos = pl.pallas_call(
        _pos_kernel,
        grid=(nc,),
        in_specs=[blk, blk, pl.BlockSpec((N_EXP_TOTAL, 1), lambda c: (0, 0))],
        out_specs=blk,
        out_shape=jax.ShapeDtypeStruct((nc, 8, RANK_TC), jnp.int32),
        name="moe_pos",
    )(ids_t, rank, (tile_start * T_EXP).astype(F32).reshape(N_EXP_TOTAL, 1))
    n_tiles = (2 * n) // T_EXP + N_EXP_TOTAL
    tile_idx = jnp.minimum(jnp.arange(n_tiles, dtype=jnp.int32), n_used - 1)
    tile_expert = jnp.sum((tile_start[None, :] <= tile_idx[:, None]).astype(jnp.int32), axis=1) - 1
    per = RANK_TC // TC
    pos_chunks = jnp.transpose(pos[:, :2].reshape(nc, 2, per, TC), (0, 2, 1, 3)).reshape(n // TC, 1, 2 * TC)
    last_tile = jnp.where(tiles_per > 0, tile_start + tiles_per - 1, -1)
    return pos_chunks, tile_expert, n_used.reshape(1).astype(jnp.int32), n_tiles, last_tile


def _row_copy_wait(src, dst, sem, rows):
    pltpu.make_async_copy(src.at[pl.ds(0, rows)], dst.at[pl.ds(0, rows)], sem).wait()


def _dispatch_kernel(last_ref, nu_ref, pos_ref, h_ref, xs_hbm, zero_s, sem):
    @pl.when(pl.program_id(0) == 0)
    def _():
        zero_s[...] = jnp.zeros_like(zero_s)

        def zero_tile(j):
            return pltpu.make_async_copy(zero_s, xs_hbm.at[pl.ds(pl.multiple_of(j * T_EXP, T_EXP), T_EXP)], sem)

        for e in range(N_EXP_TOTAL):
            @pl.when(last_ref[e] >= 0)
            def _():
                zero_tile(last_ref[e]).start()
        for e in range(N_EXP_TOTAL):
            @pl.when(last_ref[e] >= 0)
            def _():
                zero_tile(0).wait()

        def unused_tile(j, carry):
            tile = zero_tile(j)
            tile.start()
            tile.wait()
            return carry

        lax.fori_loop(nu_ref[0], xs_hbm.shape[0] // T_EXP, unused_tile, 0)

    def issue(i, carry):
        for k in range(2):
            dst = pos_ref[0, 0, k * TC + i]
            pltpu.make_async_copy(h_ref.at[pl.ds(i, 1)], xs_hbm.at[pl.ds(dst, 1)], sem).start()
        return carry

    lax.fori_loop(0, TC, issue, 0, unroll=ISSUE_UNROLL)
    for k in range(2):
        _row_copy_wait(h_ref, xs_hbm, sem, TC)


def _dispatch(h2_flat, pos_chunks, last_tile, n_used, n_rows):
    n, d = h2_flat.shape
    grid_spec = pltpu.PrefetchScalarGridSpec(
        num_scalar_prefetch=2,
        grid=(n // TC,),
        in_specs=[
            pl.BlockSpec((1, 1, 2 * TC), lambda c, pad, nu: (c, 0, 0), memory_space=pltpu.SMEM),
            pl.BlockSpec((TC, d), lambda c, pad, nu: (c, 0)),
        ],
        out_specs=pl.BlockSpec(memory_space=pl.ANY),
        scratch_shapes=[pltpu.VMEM((T_EXP, d), h2_flat.dtype), pltpu.SemaphoreType.DMA(())],
    )
    return pl.pallas_call(
        _dispatch_kernel,
        grid_spec=grid_spec,
        out_shape=jax.ShapeDtypeStruct((n_rows, d), h2_flat.dtype),
        name="moe_dispatch",
    )(last_tile, n_used, pos_chunks, h2_flat)


def _expert_kernel(te_ref, nu_ref, xs_ref, w1_ref, w3_ref, w2_ref, ys_ref, w1_s, w3_s, w2_s):
    j = pl.program_id(0)
    prev = te_ref[jnp.maximum(j - 1, 0)]

    @pl.when((j == 0) | (te_ref[j] != prev))
    def _():
        w1_s[...] = w1_ref[0].astype(BF16)
        w3_s[...] = w3_ref[0].astype(BF16)
        w2_s[...] = w2_ref[0].astype(BF16)

    @pl.when(j < nu_ref[0])
    def _():
        xb = xs_ref[...].astype(BF16)
        a = _dot(xb, w1_s[...])
        b = _dot(xb, w3_s[...])
        hid = a * jax.nn.sigmoid(a) * b
        ys_ref[...] = _dot(hid.astype(BF16), w2_s[...])

    @pl.when(j >= nu_ref[0])
    def _():
        ys_ref[...] = jnp.zeros_like(ys_ref)


def _experts(xs, n_tiles, tile_expert, n_used, w1, w3, w2, layer):
    d = xs.shape[1]
    n_rows = n_tiles * T_EXP
    f = w1.shape[-1]
    n_all = w1.shape[0] * N_EXP_TOTAL
    tile_expert = tile_expert + layer * N_EXP_TOTAL
    grid_spec = pltpu.PrefetchScalarGridSpec(
        num_scalar_prefetch=2,
        grid=(n_tiles,),
        in_specs=[
            pl.BlockSpec((T_EXP, d), lambda j, te, nu: (jnp.minimum(j, nu[0] - 1), 0)),
            pl.BlockSpec((1, d, f), lambda j, te, nu: (te[j], 0, 0)),
            pl.BlockSpec((1, d, f), lambda j, te, nu: (te[j], 0, 0)),
            pl.BlockSpec((1, f, d), lambda j, te, nu: (te[j], 0, 0)),
        ],
        out_specs=pl.BlockSpec((T_EXP, d), lambda j, te, nu: (j, 0)),
        scratch_shapes=[pltpu.VMEM((d, f), BF16), pltpu.VMEM((d, f), BF16), pltpu.VMEM((f, d), BF16)],
    )
    return pl.pallas_call(
        _expert_kernel,
        grid_spec=grid_spec,
        out_shape=jax.ShapeDtypeStruct((n_rows, d), F32),
        name="moe_experts",
    )(tile_expert, n_used, xs, w1.reshape(n_all, d, f), w3.reshape(n_all, d, f), w2.reshape(n_all, f, d))


def _combine_kernel(pos_ref, ys_hbm, x1_ref, wts_ref, mod_ref, x2_o, buf, sem):
    def issue(i, carry):
        for k in range(2):
            src = pos_ref[0, 0, k * TC + i]
            pltpu.make_async_copy(ys_hbm.at[pl.ds(src, 1)], buf.at[k, pl.ds(i, 1)], sem).start()
        return carry

    lax.fori_loop(0, TC, issue, 0, unroll=ISSUE_UNROLL)
    for k in range(2):
        _row_copy_wait(ys_hbm, buf.at[k], sem, TC)
    wts = wts_ref[0]
    y = wts[:, 0:1] * buf[0] + wts[:, 1:2] * buf[1]
    gate2 = mod_ref[0][5:6]
    x2_o[0] = x1_ref[0] + gate2 * y


def _combine(ys, pos_chunks, x1, wts, mod):
    bsz, seq, d = x1.shape
    per_b = seq // TC
    return pl.pallas_call(
        _combine_kernel,
        grid=(bsz, per_b),
        in_specs=[
            pl.BlockSpec((1, 1, 2 * TC), lambda b, s: (b * per_b + s, 0, 0), memory_space=pltpu.SMEM),
            pl.BlockSpec(memory_space=pl.ANY),
            pl.BlockSpec((1, TC, d), lambda b, s: (b, s, 0)),
            pl.BlockSpec((1, TC, LANES), lambda b, s: (b, s, 0)),
            pl.BlockSpec((1, 6, d), lambda b, s: (b, 0, 0)),
        ],
        out_specs=pl.BlockSpec((1, TC, d), lambda b, s: (b, s, 0)),
        out_shape=jax.ShapeDtypeStruct((bsz, seq, d), F32),
        scratch_shapes=[pltpu.VMEM((2, TC, d), F32), pltpu.SemaphoreType.DMA(())],
        name="moe_combine",
    )(pos_chunks, ys, x1, wts, mod)


def _moe(h2, ids, wts, x1, mod, w1, w3, w2, layer):
    bsz, seq, d = h2.shape
    n = bsz * seq
    pos_chunks, tile_expert, n_used, n_tiles, last_tile = _dispatch_plan(ids.reshape(n, LANES)[:, :2])
    xs = _dispatch(h2.reshape(n, d), pos_chunks, last_tile, n_used, n_tiles * T_EXP)
    ys = _experts(xs, n_tiles, tile_expert, n_used, w1, w3, w2, layer)
    return _combine(ys, pos_chunks, x1, wts, mod)


def kernel(x, c, positions, mod_w, mod_b, norm1_g, norm2_g, ev_w_in, ev_b_f, ev_qn_a, ev_kn_a, ev_lam,
           ev_subln_g, ev_qn_b, ev_kn_b, ev_w_out, od_w_in, od_w_out, moe_w_gr, moe_b_gr, moe_w_er,
           moe_b_er, moe_w1, moe_w3, moe_w2):
    depth = mod_w.shape[0]
    mod = _modulation(c, mod_w, mod_b)
    cos_t, sin_t = _rope_tables(positions)
    for l in range(depth):
        if l % 2 == 0:
            e = l // 2
            lambda_init = 0.8 - 0.6 * math.exp(-0.3 * l)
            qa, ka, va, qb, kb, vb = _ln_proj_even(
                x, mod[l], norm1_g[l], ev_w_in[e], ev_b_f[e], ev_qn_a[e], ev_kn_a[e], ev_qn_b[e],
                ev_kn_b[e], cos_t, sin_t)
            o = _even_mixer_attention(qa, ka, va, qb, kb, vb, ev_lam[e], ev_subln_g[e], lambda_init)
            w_out = ev_w_out[e]
        else:
            od = l // 2
            q, k, v = _ln_proj_odd(x, mod[l], norm1_g[l], od_w_in[od])
            o = _sb_attention(q, k, v)
            w_out = od_w_out[od]
        x1, h2, ids, wts = _out_proj_route(o, x, mod[l], norm2_g[l], w_out, moe_w_gr[l], moe_b_gr[l],
                                           moe_w_er[l], moe_b_er[l])
        x = _moe(h2, ids, wts, x1, mod[l], moe_w1, moe_w3, moe_w2, l)
    return x
```

```python
import functools
import math

import numpy as np
import jax
import jax.numpy as jnp
from jax import lax
from jax.experimental import pallas as pl
from jax.experimental.pallas import tpu as pltpu

F32 = jnp.float32
BF16 = jnp.bfloat16

D_MODEL = 1024
HEAD_DIM = 64
HALF = HEAD_DIM // 2
A_HEADS = 4
B_HEADS = 8
C_HEADS = 16
SEC = 512
N_GROUPS = 4
N_EXPERTS = 8
N_EXP_TOTAL = N_GROUPS * N_EXPERTS
ROPE_THETA = 10000.0
RMS_EPS = 1e-6
QK_SCALE = HEAD_DIM ** -0.5
LOG2E = math.log2(math.e)

LANES = 128
TM = 1024
TQ_SOFTMAX = 4096
TQ_STICK = 1024
TK_SOFTMAX = 1024
TK_STICK = 512
SB_BLOCK = 256
SB_HEAD = 256
DEAD_BITS = 160.0
FOX_AUG = 16
FOX_K = HEAD_DIM + FOX_AUG
T_EXP = 512
TC = 1024
RANK_TC = 1024
ISSUE_UNROLL = 32
NEG = -1e30
ROUTE_E0 = 8

NT_DIMS = (((1,), (1,)), ((), ()))
TN_DIMS = (((0,), (0,)), ((), ()))


def _dot(a, b):
    return jnp.dot(a, b, preferred_element_type=F32)


def _dot_nt(a, b):
    return lax.dot_general(a, b, NT_DIMS, preferred_element_type=F32)


def _dot_tn(a, b):
    return lax.dot_general(a, b, TN_DIMS, preferred_element_type=F32)


def _split2(x):
    hi = x.astype(BF16)
    lo = (x - hi.astype(F32)).astype(BF16)
    return hi, lo


def _split3(x):
    hi = x.astype(BF16)
    r = x - hi.astype(F32)
    mid = r.astype(BF16)
    lo = (r - mid.astype(F32)).astype(BF16)
    return hi, mid, lo


def _softplus_neg_abs(z):
    return jnp.log(1.0 + jnp.exp(-jnp.abs(z)))


def _log_sigmoid(z):
    return jnp.minimum(z, 0.0) - _softplus_neg_abs(z)


def _rms_rows(x, eps=RMS_EPS):
    return x * lax.rsqrt(jnp.mean(x * x, axis=-1, keepdims=True) + eps)


def _mod_kernel(c_ref, w_ref, b_ref, o_ref):
    c = c_ref[...]
    ca = c * jax.nn.sigmoid(c)
    c_hi, c_mid, c_lo = _split3(ca)
    w = w_ref[0]
    w_hi, w_lo = _split2(w)
    acc = _dot(c_hi, w_hi) + _dot(c_hi, w_lo) + _dot(c_mid, w_hi) + _dot(c_lo, w_hi) + _dot(c_mid, w_lo)
    o_ref[0] = acc + b_ref[0]


def _modulation(c, mod_w, mod_b):
    depth, d, n6 = mod_w.shape
    bsz = c.shape[0]
    rows = 8
    tn = 1536
    c_pad = jnp.zeros((rows, d), F32).at[:bsz].set(c)
    out = pl.pallas_call(
        _mod_kernel,
        grid=(depth, n6 // tn),
        in_specs=[
            pl.BlockSpec((rows, d), lambda l, j: (0, 0)),
            pl.BlockSpec((1, d, tn), lambda l, j: (l, 0, j)),
            pl.BlockSpec((1, 1, tn), lambda l, j: (l, 0, j)),
        ],
        out_specs=pl.BlockSpec((1, rows, tn), lambda l, j: (l, 0, j)),
        out_shape=jax.ShapeDtypeStruct((depth, rows, n6), F32),
        name="adaln_mod",
    )(c_pad, mod_w, mod_b.reshape(depth, 1, n6))
    return out[:, :bsz].reshape(depth, bsz, 6, d)


def _rope_table_kernel(pos_ref, inv_ref, cos_ref, sin_ref):
    ang = pos_ref[0].astype(F32) * inv_ref[...]
    cos_ref[0] = jnp.cos(ang)
    sin_ref[0] = jnp.sin(ang)


def _rope_tables(positions):
    bsz, seq = positions.shape
    ts = 2048
    inv = ROPE_THETA ** (-2.0 * jnp.arange(HALF, dtype=F32) / HEAD_DIM)
    return pl.pallas_call(
        _rope_table_kernel,
        grid=(bsz, seq // ts),
        in_specs=[
            pl.BlockSpec((1, 1, ts), lambda b, s: (b, 0, s)),
            pl.BlockSpec((HALF, 1), lambda b, s: (0, 0)),
        ],
        out_specs=[pl.BlockSpec((1, HALF, ts), lambda b, s: (b, 0, s))] * 2,
        out_shape=[jax.ShapeDtypeStruct((bsz, HALF, seq), F32)] * 2,
        name="rope_tables",
    )(positions.reshape(bsz, 1, seq), inv.reshape(HALF, 1))


def _adaln(x, mod_rows, g, first):
    shift = mod_rows[first:first + 1]
    scale = mod_rows[first + 1:first + 2]
    return _rms_rows(x) * g * (1.0 + scale) + shift


def _ln_proj_even_kernel(x_ref, mod_ref, g_ref, wT_ref, wfT_ref, bf_ref,
                         qna_ref, kna_ref, qnb_ref, knb_ref, cos_ref, sin_ref,
                         qa_o, ka_o, va_o, qb_o, kb_o, vb_o, carry):
    tm = x_ref.shape[1]

    @pl.when(pl.program_id(1) == 0)
    def _():
        carry[...] = jnp.zeros_like(carry)

    hb = _adaln(x_ref[0], mod_ref[0], g_ref[...], 0).astype(BF16)
    cos = cos_ref[0]
    sin = sin_ref[0]

    def section(idx):
        return _dot_nt(wT_ref[idx * SEC:(idx + 1) * SEC, :], hb)

    def norm_heads(p, g_col, out_ref, rope, scale, stride):
        for j in range(SEC // HEAD_DIM):
            xj = p[j * HEAD_DIM:(j + 1) * HEAD_DIM]
            yj = xj * lax.rsqrt(jnp.mean(xj * xj, axis=0, keepdims=True) + RMS_EPS) * g_col
            if rope:
                y1 = yj[:HALF]
                y2 = yj[HALF:]
                yj = jnp.concatenate([y1 * cos - y2 * sin, y2 * cos + y1 * sin], axis=0)
            out_ref[0, j * stride:j * stride + HEAD_DIM, :] = (yj * scale).astype(out_ref.dtype)

    softmax_q_scale = QK_SCALE * LOG2E
    norm_heads(section(0), qna_ref[...], qa_o, True, softmax_q_scale, HEAD_DIM)
    norm_heads(section(1), kna_ref[...], ka_o, True, 1.0, HEAD_DIM)
    va_o[0] = section(2).astype(va_o.dtype)
    norm_heads(section(3), qnb_ref[...], qb_o, False, softmax_q_scale, FOX_K)
    norm_heads(section(4), knb_ref[...], kb_o, False, 1.0, FOX_K)
    vb_o[0] = section(5).astype(vb_o.dtype)

    r = lax.broadcasted_iota(jnp.int32, (tm, tm), 0)
    c = lax.broadcasted_iota(jnp.int32, (tm, tm), 1)
    upto = jnp.where(r <= c, 1.0, 0.0).astype(BF16)
    log_f = _log_sigmoid(_dot_nt(wfT_ref[...], hb)[:B_HEADS] + bf_ref[...])
    f_hi, f_mid, f_lo = _split3(log_f)
    cum = _dot(f_hi, upto) + _dot(f_mid, upto) + _dot(f_lo, upto) + carry[...]
    carry[...] = cum[:, tm - 1:tm]

    row = lax.broadcasted_iota(jnp.int32, (FOX_AUG, tm), 0)
    for j in range(B_HEADS):
        pieces = [p.astype(F32) for p in _split3(cum[j:j + 1] * LOG2E)]
        q_aug = jnp.where(row < 3, 1.0, 0.0)
        k_aug = jnp.where((row >= 3) & (row < 6), 1.0, 0.0)
        for i, piece in enumerate(pieces):
            q_aug = jnp.where(row == 3 + i, piece, q_aug)
            k_aug = jnp.where(row == i, -piece, k_aug)
        lo = j * FOX_K + HEAD_DIM
        qb_o[0, lo:lo + FOX_AUG, :] = q_aug.astype(qb_o.dtype)
        kb_o[0, lo:lo + FOX_AUG, :] = k_aug.astype(kb_o.dtype)


def _ln_proj_even(x, mod, g, w_in, b_f, qn_a, kn_a, qn_b, kn_b, cos_t, sin_t):
    bsz, seq, d = x.shape
    n_main = 6 * SEC
    wT = w_in[:, :n_main].T.astype(BF16)
    wfT = jnp.zeros((16, d), F32).at[:B_HEADS].set(w_in[:, n_main:].T).astype(BF16)
    col = lambda v: v.reshape(HEAD_DIM, 1)
    full = lambda shape: pl.BlockSpec(shape, lambda b, s: (0,) * len(shape))
    rows_spec = lambda rows: pl.BlockSpec((1, rows, TM), lambda b, s: (b, 0, s))
    rows_shape = lambda rows: jax.ShapeDtypeStruct((bsz, rows, seq), BF16)
    out_rows = [SEC, SEC, SEC, B_HEADS * FOX_K, B_HEADS * FOX_K, SEC]
    return pl.pallas_call(
        _ln_proj_even_kernel,
        grid=(bsz, seq // TM),
        in_specs=[
            pl.BlockSpec((1, TM, d), lambda b, s: (b, s, 0)),
            pl.BlockSpec((1, 6, d), lambda b, s: (b, 0, 0)),
            full((1, d)), full((n_main, d)), full((16, d)), full((B_HEADS, 1)),
            full((HEAD_DIM, 1)), full((HEAD_DIM, 1)), full((HEAD_DIM, 1)), full((HEAD_DIM, 1)),
            rows_spec(HALF), rows_spec(HALF),
        ],
        out_specs=[rows_spec(n) for n in out_rows],
        out_shape=[rows_shape(n) for n in out_rows],
        scratch_shapes=[pltpu.VMEM((B_HEADS, 1), F32)],
        name="ln_proj_even",
    )(x, mod, g.reshape(1, d), wT, wfT, b_f.reshape(B_HEADS, 1), col(qn_a), col(kn_a), col(qn_b),
      col(kn_b), cos_t, sin_t)


def _ln_proj_odd_kernel(x_ref, mod_ref, g_ref, wT_ref, q_o, k_o, v_o):
    hb = _adaln(x_ref[0], mod_ref[0], g_ref[...], 0).astype(BF16)
    width = q_o.shape[1]
    for idx, (out_ref, scale) in enumerate(((q_o, QK_SCALE * LOG2E), (k_o, 1.0), (v_o, 1.0))):
        for half in range(2):
            lo = idx * width + half * (width // 2)
            p = _dot_nt(wT_ref[lo:lo + width // 2, :], hb)
            out_ref[0, half * (width // 2):(half + 1) * (width // 2), :] = (p * scale).astype(out_ref.dtype)


def _ln_proj_odd(x, mod, g, w_in):
    bsz, seq, d = x.shape
    width = C_HEADS * HEAD_DIM
    wT = w_in.T.astype(BF16)
    full = lambda shape: pl.BlockSpec(shape, lambda b, s: (0,) * len(shape))
    spec = pl.BlockSpec((1, width, TM), lambda b, s: (b, 0, s))
    shape = jax.ShapeDtypeStruct((bsz, width, seq), BF16)
    return pl.pallas_call(
        _ln_proj_odd_kernel,
        grid=(bsz, seq // TM),
        in_specs=[
            pl.BlockSpec((1, TM, d), lambda b, s: (b, s, 0)),
            pl.BlockSpec((1, 6, d), lambda b, s: (b, 0, 0)),
            full((1, d)), full((3 * width, d)),
        ],
        out_specs=[spec] * 3,
        out_shape=[shape] * 3,
        name="ln_proj_odd",
    )(x, mod, g.reshape(1, d), wT)


def _diag_valid(shape, strict):
    key = lax.broadcasted_iota(jnp.int32, shape, 0)
    qry = lax.broadcasted_iota(jnp.int32, shape, 1)
    return (qry > key) if strict else (qry >= key)


def _mask_triangle(x, strict, fill):
    tk = x.shape[0]
    square = jnp.where(_diag_valid((tk, tk), strict), x[:, :tk], fill)
    return square if x.shape[1] == tk else jnp.concatenate([square, x[:, tk:]], axis=1)


def _softmax_tile(kT, qT_ref, q_rows, vT, m_ref, l_ref, acc_ref, diag):
    qs = slice(0 if diag is None else diag, None)
    sT = _dot_tn(kT, qT_ref[0, q_rows, qs])
    if diag is not None:
        sT = _mask_triangle(sT, False, NEG)
    m_prev = m_ref[:, qs]
    m_new = jnp.maximum(m_prev, jnp.max(sT, axis=0, keepdims=True))
    alpha = jnp.exp2(m_prev - m_new)
    p = jnp.exp2(sT - m_new)
    l_ref[:, qs] = alpha * l_ref[:, qs] + jnp.sum(p, axis=0, keepdims=True)
    acc_ref[:, qs] = alpha * acc_ref[:, qs] + _dot(vT, p.astype(BF16))
    m_ref[:, qs] = m_new


def _init_softmax(m_refs, l_refs, acc_refs):
    for r in m_refs:
        r[...] = jnp.full_like(r, -jnp.inf)
    for r in l_refs + acc_refs:
        r[...] = jnp.zeros_like(r)


def _key_tile(ki, tk):
    return pl.ds(pl.multiple_of(ki * tk, tk), tk)


def _attn_diff_kernel(qT_ref, kT_ref, vT_ref, lam_ref, sg_ref, o_ref,
                      m0, m1, l0, l1, acc0, acc1, *, lambda_init):
    qi = pl.program_id(2)
    ms, ls, accs = [m0, m1], [l0, l1], [acc0, acc1]
    _init_softmax(ms, ls, accs)

    def tiles(ki, diag):
        cols = _key_tile(ki, TK_SOFTMAX)
        for c in range(2):
            hs = slice(c * HEAD_DIM, (c + 1) * HEAD_DIM)
            _softmax_tile(kT_ref[0, hs, cols], qT_ref, hs, vT_ref[0, :, cols], ms[c], ls[c], accs[c], diag)

    def below_diagonal(ki, carry):
        tiles(ki, None)
        return carry

    diag_tiles = qT_ref.shape[2] // TK_SOFTMAX
    n_below = qi * diag_tiles
    lax.fori_loop(0, n_below, below_diagonal, 0)
    for d in range(diag_tiles):
        tiles(n_below + d, d * TK_SOFTMAX)
    lam = lam_ref[...]
    e1 = jnp.exp(jnp.sum(lam[0:1] * lam[1:2], axis=-1, keepdims=True))
    e2 = jnp.exp(jnp.sum(lam[2:3] * lam[3:4], axis=-1, keepdims=True))
    lam_full = e1 - e2 + lambda_init
    oT = acc0[...] * (1.0 / l0[...]) - lam_full * (acc1[...] * (1.0 / l1[...]))
    oT = oT * lax.rsqrt(jnp.mean(oT * oT, axis=0, keepdims=True) + RMS_EPS)
    oT = oT * sg_ref[...] * (1.0 - lambda_init)
    o_ref[0] = oT.T.astype(o_ref.dtype)


def _attn_fox_kernel(qT_ref, kT_ref, vT_ref, o_ref, m0, m1, l0, l1, acc0, acc1):
    qi = pl.program_id(2)
    ms, ls, accs = [m0, m1], [l0, l1], [acc0, acc1]
    _init_softmax(ms, ls, accs)

    def tiles(ki, diag):
        cols = _key_tile(ki, TK_SOFTMAX)
        for j in range(2):
            ks = slice(j * FOX_K, (j + 1) * FOX_K)
            vs = slice(j * HEAD_DIM, (j + 1) * HEAD_DIM)
            _softmax_tile(kT_ref[0, ks, cols], qT_ref, ks, vT_ref[0, vs, cols], ms[j], ls[j], accs[j], diag)

    def below_diagonal(ki, carry):
        tiles(ki, None)
        return carry

    diag_tiles = qT_ref.shape[2] // TK_SOFTMAX
    n_below = qi * diag_tiles
    lax.fori_loop(0, n_below, below_diagonal, 0)
    for d in range(diag_tiles):
        tiles(n_below + d, d * TK_SOFTMAX)
    oT = jnp.concatenate([acc0[...] * (1.0 / l0[...]), acc1[...] * (1.0 / l1[...])], axis=0)
    o_ref[0] = oT.T.astype(o_ref.dtype)


def _sb_tile(kT, qT_ref, q_rows, vT, later_ref, r_ref, acc_ref, diag, q_stop=None):
    qs = slice(0 if diag is None else diag, q_stop)
    zT = _dot_tn(kT, qT_ref[0, q_rows, qs])
    neg_keep = jnp.maximum(zT, 0.0) + jnp.log2(1.0 + jnp.exp2(-jnp.abs(zT)))
    log_beta = zT - neg_keep
    if diag is not None:
        neg_keep = _mask_triangle(neg_keep, True, 0.0)
    keep_bf = neg_keep.astype(BF16)
    beyond = r_ref[:, qs]
    parts = []
    for blk in reversed(range(zT.shape[0] // SB_BLOCK)):
        rows = slice(blk * SB_BLOCK, (blk + 1) * SB_BLOCK)
        within = _dot(later_ref[...], keep_bf[rows])
        parts.append(within + beyond)
        beyond = beyond + within[0:1] + neg_keep[blk * SB_BLOCK:blk * SB_BLOCK + 1]
    laterT = jnp.concatenate(parts[::-1], axis=0)
    a = jnp.exp2(log_beta - laterT)
    if diag is not None:
        a = _mask_triangle(a, True, 0.0)
    acc_ref[:, qs] = acc_ref[:, qs] + _dot(vT, a.astype(BF16))
    r_ref[:, qs] = beyond


def _attn_sb_kernel(qT_ref, kT_ref, vT_ref, later_ref, o_ref, r0, r1, acc0, acc1):
    qi = pl.program_id(2)
    rs, accs = [r0, r1], [acc0, acc1]
    for r in rs + accs:
        r[...] = jnp.zeros_like(r)

    def tiles(ki, diag, q_stop=None):
        cols = _key_tile(ki, TK_STICK)
        for j in range(2):
            hs = slice(j * HEAD_DIM, (j + 1) * HEAD_DIM)
            _sb_tile(kT_ref[0, hs, cols], qT_ref, hs, vT_ref[0, hs, cols], later_ref, rs[j], accs[j], diag,
                     q_stop)

    tq = qT_ref.shape[2]
    diag_tiles = tq // TK_STICK
    n_below = qi * diag_tiles
    head = min(SB_HEAD, tq)

    def least_decay(lo, hi):
        return jnp.minimum(jnp.min(r0[:, lo:hi]), jnp.min(r1[:, lo:hi]))

    def more_to_do(state):
        i, head_decay, rest_decay = state
        return (i < n_below) & (jnp.minimum(head_decay, rest_decay) < DEAD_BITS)

    def below_diagonal(state):
        i, _, rest_decay = state
        ki = n_below - 1 - i
        if head < tq:
            @pl.when(rest_decay >= DEAD_BITS)
            def _():
                tiles(ki, None, head)

            @pl.when(rest_decay < DEAD_BITS)
            def _():
                tiles(ki, None)
        else:
            tiles(ki, None)
        return i + 1, least_decay(0, head), (least_decay(head, tq) if head < tq else jnp.float32(DEAD_BITS))

    for d in reversed(range(diag_tiles)):
        tiles(n_below + d, d * TK_STICK)
    start = (0, least_decay(0, head), least_decay(head, tq) if head < tq else jnp.float32(DEAD_BITS))
    lax.while_loop(more_to_do, below_diagonal, start)
    o_ref[0] = jnp.concatenate([acc0[...], acc1[...]], axis=0).T.astype(o_ref.dtype)


def _attention_call(body, name, n_groups, tq, qT, kT, vT, qk_rows, v_rows, extra_in, extra_specs, n_stats,
                    v_dim):
    bsz, _, seq = qT.shape
    in_specs = [
        pl.BlockSpec((1, qk_rows, tq), lambda b, h, qi: (b, h, qi)),
        pl.BlockSpec((1, qk_rows, seq), lambda b, h, qi: (b, h, 0)),
        pl.BlockSpec((1, v_rows, seq), lambda b, h, qi: (b, h, 0)),
    ] + extra_specs
    scratch = [pltpu.VMEM((1, tq), F32)] * n_stats + [pltpu.VMEM((v_dim, tq), F32)] * 2
    return pl.pallas_call(
        body,
        grid=(bsz, n_groups, seq // tq),
        in_specs=in_specs,
        out_specs=pl.BlockSpec((1, tq, 2 * HEAD_DIM), lambda b, h, qi: (b, qi, h)),
        out_shape=jax.ShapeDtypeStruct((bsz, seq, n_groups * 2 * HEAD_DIM), BF16),
        scratch_shapes=scratch,
        name=name,
    )(qT, kT, vT, *extra_in)


def _even_mixer_attention(qa, ka, va, qb, kb, vb, lam, subln_g, lambda_init):
    full = lambda shape: pl.BlockSpec(shape, lambda b, h, qi: (0,) * len(shape))
    pair = 2 * HEAD_DIM
    o_a = _attention_call(
        functools.partial(_attn_diff_kernel, lambda_init=lambda_init), "attn_diff", A_HEADS, TQ_SOFTMAX,
        qa, ka, va, pair, pair, [lam, subln_g.reshape(pair, 1)],
        [full((4, HEAD_DIM)), full((pair, 1))], 4, pair)
    o_b = _attention_call(_attn_fox_kernel, "attn_fox", B_HEADS // 2, TQ_SOFTMAX, qb, kb, vb, 2 * FOX_K, pair,
                          [], [], 4, HEAD_DIM)
    return [o_a, o_b]


def _sb_attention(qT, kT, vT):
    full = lambda shape: pl.BlockSpec(shape, lambda b, h, qi: (0,) * len(shape))
    pair = 2 * HEAD_DIM
    later = jnp.triu(jnp.ones((SB_BLOCK, SB_BLOCK), F32), 1).astype(BF16)
    return [_attention_call(_attn_sb_kernel, "attn_sb", C_HEADS // 2, TQ_STICK, qT, kT, vT, pair, pair,
                            [later], [full((SB_BLOCK, SB_BLOCK))], 2, HEAD_DIM)]


def _out_proj_route_kernel(*refs, n_parts):
    o_refs = refs[:n_parts]
    wo_refs = refs[n_parts:2 * n_parts]
    x_ref, mod_ref, g_ref, wrh_ref, wrl_ref, br_ref, x1_o, h2_o, ids_o, wts_o = refs[2 * n_parts:]
    mod_rows = mod_ref[0]
    gate1 = mod_rows[2:3]
    mix = _dot(o_refs[0][0], wo_refs[0][...])
    for o_ref, wo_ref in zip(o_refs[1:], wo_refs[1:]):
        mix = mix + _dot(o_ref[0], wo_ref[...])
    x1 = x_ref[0] + gate1 * mix
    x1_o[0] = x1
    h2 = _adaln(x1, mod_rows, g_ref[...], 3)
    h2_o[0] = h2

    h_hi, h_lo = _split2(h2)
    wr_hi = wrh_ref[...]
    logits = _dot(h_hi, wr_hi) + _dot(h_hi, wrl_ref[...]) + _dot(h_lo, wr_hi) + br_ref[...]

    lane = lax.broadcasted_iota(jnp.int32, logits.shape, 1)
    big = jnp.int32(LANES)
    g_mask = lane < N_GROUPS
    g_log = jnp.where(g_mask, logits, NEG)
    g_max = jnp.max(g_log, axis=-1, keepdims=True)
    g_sum = jnp.sum(jnp.where(g_mask, jnp.exp(g_log - g_max), 0.0), axis=-1, keepdims=True)
    g_w = 1.0 / g_sum
    g_idx = jnp.min(jnp.where(g_mask & (g_log == g_max), lane, big), axis=-1, keepdims=True)

    e_mask = (lane >= ROUTE_E0) & (lane < ROUTE_E0 + N_EXP_TOTAL) & ((lane >> 3) == g_idx + 1)
    e_log = jnp.where(e_mask, logits, NEG)
    e_max = jnp.max(e_log, axis=-1, keepdims=True)
    e_exp = jnp.where(e_mask, jnp.exp(e_log - e_max), 0.0)
    e_prob = e_exp / jnp.sum(e_exp, axis=-1, keepdims=True)
    p1 = jnp.max(jnp.where(e_mask, e_prob, -1.0), axis=-1, keepdims=True)
    i1 = jnp.min(jnp.where(e_mask & (e_prob == p1), lane, big), axis=-1, keepdims=True)
    rest = e_mask & (lane != i1)
    p2 = jnp.max(jnp.where(rest, e_prob, -1.0), axis=-1, keepdims=True)
    i2 = jnp.min(jnp.where(rest & (e_prob == p2), lane, big), axis=-1, keepdims=True)
    den = p1 + p2
    w1 = p1 / den * g_w
    w2 = p2 / den * g_w
    ids_o[0] = jnp.where(lane == 0, i1 - ROUTE_E0, jnp.where(lane == 1, i2 - ROUTE_E0, 0))
    wts_o[0] = jnp.where(lane == 0, w1, jnp.where(lane == 1, w2, 0.0))


def _out_proj_route(o_parts, x, mod, g2, w_out, w_gr, b_gr, w_er, b_er):
    bsz, seq, d = x.shape
    widths = [o.shape[-1] for o in o_parts]
    starts = np.cumsum([0] + widths)
    wo_parts = [w_out[starts[i]:starts[i + 1]].astype(BF16) for i in range(len(o_parts))]
    wr = jnp.zeros((d, LANES), F32)
    wr = wr.at[:, :N_GROUPS].set(w_gr)
    wr = wr.at[:, ROUTE_E0:ROUTE_E0 + N_EXP_TOTAL].set(
        jnp.transpose(w_er, (1, 0, 2)).reshape(d, N_EXP_TOTAL))
    br = jnp.zeros((1, LANES), F32)
    br = br.at[0, :N_GROUPS].set(b_gr)
    br = br.at[0, ROUTE_E0:ROUTE_E0 + N_EXP_TOTAL].set(b_er.reshape(N_EXP_TOTAL))
    wr_hi = wr.astype(BF16)
    wr_lo = (wr - wr_hi.astype(F32)).astype(BF16)
    full = lambda shape: pl.BlockSpec(shape, lambda b, s: (0,) * len(shape))
    tile = lambda w: pl.BlockSpec((1, TM, w), lambda b, s: (b, s, 0))
    return pl.pallas_call(
        functools.partial(_out_proj_route_kernel, n_parts=len(o_parts)),
        grid=(bsz, seq // TM),
        in_specs=[tile(w) for w in widths] + [full((w, d)) for w in widths] + [
            tile(d),
            pl.BlockSpec((1, 6, d), lambda b, s: (b, 0, 0)),
            full((1, d)), full((d, LANES)), full((d, LANES)), full((1, LANES)),
        ],
        out_specs=[tile(d), tile(d), tile(LANES), tile(LANES)],
        out_shape=[
            jax.ShapeDtypeStruct((bsz, seq, d), F32),
            jax.ShapeDtypeStruct((bsz, seq, d), F32),
            jax.ShapeDtypeStruct((bsz, seq, LANES), jnp.int32),
            jax.ShapeDtypeStruct((bsz, seq, LANES), F32),
        ],
        name="out_proj_route",
    )(*o_parts, *wo_parts, x, mod, g2.reshape(1, d), wr_hi, wr_lo, br)


def _expert_onehots(ids_ref):
    tc = ids_ref.shape[2]
    expert = lax.broadcasted_iota(jnp.int32, (N_EXP_TOTAL, tc), 0)
    ids = ids_ref[0]
    return [jnp.where(expert == ids[k:k + 1], 1.0, 0.0) for k in range(2)]


def _rank_kernel(ids_ref, rank_o, cnt_o, carry):
    tc = ids_ref.shape[2]

    @pl.when(pl.program_id(0) == 0)
    def _():
        carry[...] = jnp.zeros_like(carry)

    r = lax.broadcasted_iota(jnp.int32, (tc, tc), 0)
    c = lax.broadcasted_iota(jnp.int32, (tc, tc), 1)
    before = jnp.where(r < c, 1.0, 0.0).astype(BF16)
    base = carry[...]
    rank_o[0] = jnp.zeros(rank_o.shape[1:], rank_o.dtype)
    for k, oh in enumerate(_expert_onehots(ids_ref)):
        prefix = _dot(oh.astype(BF16), before)
        rank_o[0, k:k + 1, :] = jnp.sum(oh * (base + prefix), axis=0, keepdims=True).astype(jnp.int32)
        base = base + jnp.sum(oh, axis=-1, keepdims=True)
    carry[...] = base
    cnt_o[...] = jnp.broadcast_to(base, cnt_o.shape)


def _pos_kernel(ids_ref, rank_ref, off_ref, pos_o):
    rank = rank_ref[0]
    pos_o[0] = jnp.zeros(pos_o.shape[1:], pos_o.dtype)
    for k, oh in enumerate(_expert_onehots(ids_ref)):
        off = jnp.sum(oh * off_ref[...], axis=0, keepdims=True).astype(jnp.int32)
        pos_o[0, k:k + 1, :] = off + rank[k:k + 1]


def _dispatch_plan(ids2):
    n = ids2.shape[0]
    nc = n // RANK_TC
    ids_t = jnp.zeros((nc, 8, RANK_TC), jnp.int32).at[:, :2].set(
        jnp.transpose(ids2.reshape(nc, RANK_TC, 2), (0, 2, 1)))
    blk = pl.BlockSpec((1, 8, RANK_TC), lambda c: (c, 0, 0))
    rank, cnt = pl.pallas_call(
        _rank_kernel,
        grid=(nc,),
        in_specs=[blk],
        out_specs=[blk, pl.BlockSpec((N_EXP_TOTAL, LANES), lambda c: (0, 0))],
        out_shape=[jax.ShapeDtypeStruct((nc, 8, RANK_TC), jnp.int32),
                   jax.ShapeDtypeStruct((N_EXP_TOTAL, LANES), F32)],
        scratch_shapes=[pltpu.VMEM((N_EXP_TOTAL, 1), F32)],
        name="moe_rank",
    )(ids_t)
    counts = cnt[:, 0].astype(jnp.int32)
    tiles_per = (counts + T_EXP - 1) // T_EXP
    tile_start = jnp.cumsum(tiles_per) - tiles_per
    n_used = jnp.sum(tiles_per)
    pos = pl.pallas_call(
        _pos_kernel,
        grid=(nc,),
        in_specs=[blk, blk, pl.BlockSpec((N_EXP_TOTAL, 1), lambda c: (0, 0))],
        out_specs=blk,
        out_shape=jax.ShapeDtypeStruct((nc, 8, RANK_TC), jnp.int32),
        name="moe_pos",
    )(ids_t, rank, (tile_start * T_EXP).astype(F32).reshape(N_EXP_TOTAL, 1))
    n_tiles = (2 * n) // T_EXP + N_EXP_TOTAL
    tile_idx = jnp.minimum(jnp.arange(n_tiles, dtype=jnp.int32), n_used - 1)
    tile_expert = jnp.sum((tile_start[None, :] <= tile_idx[:, None]).astype(jnp.int32), axis=1) - 1
    per = RANK_TC // TC
    pos_chunks = jnp.transpose(pos[:, :2].reshape(nc, 2, per, TC), (0, 2, 1, 3)).reshape(n // TC, 1, 2 * TC)
    last_tile = jnp.where(tiles_per > 0, tile_start + tiles_per - 1, -1)
    return pos_chunks, tile_expert, n_used.reshape(1).astype(jnp.int32), n_tiles, last_tile


def _row_copy_wait(src, dst, sem, rows):
    pltpu.make_async_copy(src.at[pl.ds(0, rows)], dst.at[pl.ds(0, rows)], sem).wait()


def _dispatch_kernel(last_ref, nu_ref, pos_ref, h_ref, xs_hbm, zero_s, sem):
    @pl.when(pl.program_id(0) == 0)
    def _():
        zero_s[...] = jnp.zeros_like(zero_s)

        def zero_tile(j):
            return pltpu.make_async_copy(zero_s, xs_hbm.at[pl.ds(pl.multiple_of(j * T_EXP, T_EXP), T_EXP)], sem)

        for e in range(N_EXP_TOTAL):
            @pl.when(last_ref[e] >= 0)
            def _():
                zero_tile(last_ref[e]).start()
        for e in range(N_EXP_TOTAL):
            @pl.when(last_ref[e] >= 0)
            def _():
                zero_tile(0).wait()

        def unused_tile(j, carry):
            tile = zero_tile(j)
            tile.start()
            tile.wait()
            return carry

        lax.fori_loop(nu_ref[0], xs_hbm.shape[0] // T_EXP, unused_tile, 0)

    def issue(i, carry):
        for k in range(2):
            dst = pos_ref[0, 0, k * TC + i]
            pltpu.make_async_copy(h_ref.at[pl.ds(i, 1)], xs_hbm.at[pl.ds(dst, 1)], sem).start()
        return carry

    lax.fori_loop(0, TC, issue, 0, unroll=ISSUE_UNROLL)
    for k in range(2):
        _row_copy_wait(h_ref, xs_hbm, sem, TC)


def _dispatch(h2_flat, pos_chunks, last_tile, n_used, n_rows):
    n, d = h2_flat.shape
    grid_spec = pltpu.PrefetchScalarGridSpec(
        num_scalar_prefetch=2,
        grid=(n // TC,),
        in_specs=[
            pl.BlockSpec((1, 1, 2 * TC), lambda c, pad, nu: (c, 0, 0), memory_space=pltpu.SMEM),
            pl.BlockSpec((TC, d), lambda c, pad, nu: (c, 0)),
        ],
        out_specs=pl.BlockSpec(memory_space=pl.ANY),
        scratch_shapes=[pltpu.VMEM((T_EXP, d), h2_flat.dtype), pltpu.SemaphoreType.DMA(())],
    )
    return pl.pallas_call(
        _dispatch_kernel,
        grid_spec=grid_spec,
        out_shape=jax.ShapeDtypeStruct((n_rows, d), h2_flat.dtype),
        name="moe_dispatch",
    )(last_tile, n_used, pos_chunks, h2_flat)


def _expert_kernel(te_ref, nu_ref, xs_ref, w1_ref, w3_ref, w2_ref, ys_ref, w1_s, w3_s, w2_s):
    j = pl.program_id(0)
    prev = te_ref[jnp.maximum(j - 1, 0)]

    @pl.when((j == 0) | (te_ref[j] != prev))
    def _():
        w1_s[...] = w1_ref[0].astype(BF16)
        w3_s[...] = w3_ref[0].astype(BF16)
        w2_s[...] = w2_ref[0].astype(BF16)

    @pl.when(j < nu_ref[0])
    def _():
        xb = xs_ref[...].astype(BF16)
        a = _dot(xb, w1_s[...])
        b = _dot(xb, w3_s[...])
        hid = a * jax.nn.sigmoid(a) * b
        ys_ref[...] = _dot(hid.astype(BF16), w2_s[...])

    @pl.when(j >= nu_ref[0])
    def _():
        ys_ref[...] = jnp.zeros_like(ys_ref)


def _experts(xs, n_tiles, tile_expert, n_used, w1, w3, w2, layer):
    d = xs.shape[1]
    n_rows = n_tiles * T_EXP
    f = w1.shape[-1]
    n_all = w1.shape[0] * N_EXP_TOTAL
    tile_expert = tile_expert + layer * N_EXP_TOTAL
    grid_spec = pltpu.PrefetchScalarGridSpec(
        num_scalar_prefetch=2,
        grid=(n_tiles,),
        in_specs=[
            pl.BlockSpec((T_EXP, d), lambda j, te, nu: (jnp.minimum(j, nu[0] - 1), 0)),
            pl.BlockSpec((1, d, f), lambda j, te, nu: (te[j], 0, 0)),
            pl.BlockSpec((1, d, f), lambda j, te, nu: (te[j], 0, 0)),
            pl.BlockSpec((1, f, d), lambda j, te, nu: (te[j], 0, 0)),
        ],
        out_specs=pl.BlockSpec((T_EXP, d), lambda j, te, nu: (j, 0)),
        scratch_shapes=[pltpu.VMEM((d, f), BF16), pltpu.VMEM((d, f), BF16), pltpu.VMEM((f, d), BF16)],
    )
    return pl.pallas_call(
        _expert_kernel,
        grid_spec=grid_spec,
        out_shape=jax.ShapeDtypeStruct((n_rows, d), F32),
        name="moe_experts",
    )(tile_expert, n_used, xs, w1.reshape(n_all, d, f), w3.reshape(n_all, d, f), w2.reshape(n_all, f, d))


def _combine_kernel(pos_ref, pos_next_ref, ys_hbm, x1_ref, wts_ref, mod_ref, x2_o, buf, sems):
    step = pl.program_id(0) * pl.num_programs(1) + pl.program_id(1)
    n_steps = pl.num_programs(0) * pl.num_programs(1)
    slot = step % 2

    def start_chunk(p_ref, into):
        def issue(i, carry):
            for k in range(2):
                src = p_ref[0, 0, k * TC + i]
                pltpu.make_async_copy(ys_hbm.at[pl.ds(src, 1)], buf.at[into, k, pl.ds(i, 1)],
                                      sems.at[into]).start()
            return carry

        lax.fori_loop(0, TC, issue, 0, unroll=ISSUE_UNROLL)

    @pl.when(step == 0)
    def _():
        start_chunk(pos_ref, 0)

    @pl.when(step + 1 < n_steps)
    def _():
        start_chunk(pos_next_ref, 1 - slot)

    for k in range(2):
        _row_copy_wait(ys_hbm, buf.at[slot, k], sems.at[slot], TC)
    wts = wts_ref[0]
    y = wts[:, 0:1] * buf[slot, 0] + wts[:, 1:2] * buf[slot, 1]
    gate2 = mod_ref[0][5:6]
    x2_o[0] = x1_ref[0] + gate2 * y


def _combine(ys, pos_chunks, x1, wts, mod):
    bsz, seq, d = x1.shape
    per_b = seq // TC
    last = bsz * per_b - 1
    return pl.pallas_call(
        _combine_kernel,
        grid=(bsz, per_b),
        in_specs=[
            pl.BlockSpec((1, 1, 2 * TC), lambda b, s: (b * per_b + s, 0, 0), memory_space=pltpu.SMEM),
            pl.BlockSpec((1, 1, 2 * TC), lambda b, s: (jnp.minimum(b * per_b + s + 1, last), 0, 0),
                         memory_space=pltpu.SMEM),
            pl.BlockSpec(memory_space=pl.ANY),
            pl.BlockSpec((1, TC, d), lambda b, s: (b, s, 0)),
            pl.BlockSpec((1, TC, LANES), lambda b, s: (b, s, 0)),
            pl.BlockSpec((1, 6, d), lambda b, s: (b, 0, 0)),
        ],
        out_specs=pl.BlockSpec((1, TC, d), lambda b, s: (b, s, 0)),
        out_shape=jax.ShapeDtypeStruct((bsz, seq, d), F32),
        scratch_shapes=[pltpu.VMEM((2, 2, TC, d), F32), pltpu.SemaphoreType.DMA((2,))],
        name="moe_combine",
    )(pos_chunks, pos_chunks, ys, x1, wts, mod)


def _moe(h2, ids, wts, x1, mod, w1, w3, w2, layer):
    bsz, seq, d = h2.shape
    n = bsz * seq
    pos_chunks, tile_expert, n_used, n_tiles, last_tile = _dispatch_plan(ids.reshape(n, LANES)[:, :2])
    xs = _dispatch(h2.reshape(n, d), pos_chunks, last_tile, n_used, n_tiles * T_EXP)
    ys = _experts(xs, n_tiles, tile_expert, n_used, w1, w3, w2, layer)
    return _combine(ys, pos_chunks, x1, wts, mod)


def kernel(x, c, positions, mod_w, mod_b, norm1_g, norm2_g, ev_w_in, ev_b_f, ev_qn_a, ev_kn_a, ev_lam,
           ev_subln_g, ev_qn_b, ev_kn_b, ev_w_out, od_w_in, od_w_out, moe_w_gr, moe_b_gr, moe_w_er,
           moe_b_er, moe_w1, moe_w3, moe_w2):
    depth = mod_w.shape[0]
    mod = _modulation(c, mod_w, mod_b)
    cos_t, sin_t = _rope_tables(positions)
    for l in range(depth):
        if l % 2 == 0:
            e = l // 2
            lambda_init = 0.8 - 0.6 * math.exp(-0.3 * l)
            qa, ka, va, qb, kb, vb = _ln_proj_even(
                x, mod[l], norm1_g[l], ev_w_in[e], ev_b_f[e], ev_qn_a[e], ev_kn_a[e], ev_qn_b[e],
                ev_kn_b[e], cos_t, sin_t)
            o = _even_mixer_attention(qa, ka, va, qb, kb, vb, ev_lam[e], ev_subln_g[e], lambda_init)
            w_out = ev_w_out[e]
        else:
            od = l // 2
            q, k, v = _ln_proj_odd(x, mod[l], norm1_g[l], od_w_in[od])
            o = _sb_attention(q, k, v)
            w_out = od_w_out[od]
        x1, h2, ids, wts = _out_proj_route(o, x, mod[l], norm2_g[l], w_out, moe_w_gr[l], moe_b_gr[l],
                                           moe_w_er[l], moe_b_er[l])
        x = _moe(h2, ids, wts, x1, mod[l], moe_w1, moe_w3, moe_w2, l)
    return x
```
